```python
import jax, jax.numpy as jnp
from jax import lax
import numpy as np

D_MODEL = 1024
BATCH = 16
SEQ = 4096
DEPTH = 1
DEC_BATCH = 8
DEC_SEQ = 32
PAST_LEN = 2048

CHUNK = 64
N_META = 16
ROPE_THETA = 500000.0
EPS = 1e-6
QBLOCK = 128

A_HEADS = 8
A_NOPE = 64
A_ROPE = 32
A_V = 64
A_QRANK = 256
A_KVRANK = 128
A_SCALE = (A_NOPE + A_ROPE) ** -0.5

B_HEADS = 8
B_HEAD_DIM = 64
B_ROT = B_HEAD_DIM // 4
B_SCALE = B_HEAD_DIM ** -0.5
IDX_HEADS = 8
IDX_DIM = 64
IDX_ROT = IDX_DIM // 4
IDX_TOPK = 256
IDX_W_SCALE = (IDX_HEADS * IDX_DIM) ** -0.5

PEER_HEADS = 8
PEER_DKEY = 128
N_KEYS = 128
N_EXPERTS = N_KEYS * N_KEYS
PEER_TOPK = 16
PEER_BLOCK = 256

IN_SPLITS = (A_QRANK, A_KVRANK, A_ROPE, B_HEADS * B_HEAD_DIM, B_HEAD_DIM, B_HEAD_DIM,
             IDX_HEADS * IDX_DIM, IDX_HEADS, IDX_DIM, D_MODEL, D_MODEL)
IN_COLS = A_QRANK + A_KVRANK + A_ROPE + B_HEADS * B_HEAD_DIM + 2 * B_HEAD_DIM + IDX_HEADS * IDX_DIM + IDX_HEADS + IDX_DIM + 2 * D_MODEL

kernel_name = 'hybrid_mla_dsa_peer_stream_step'


def _rms(x, g):
    xf = x.astype(jnp.float32)
    y = xf * lax.rsqrt(jnp.mean(xf * xf, axis=-1, keepdims=True) + EPS)
    return (y * g.astype(jnp.float32)).astype(x.dtype)


def _rotary(x, pos, rot_dim):
    half = rot_dim // 2
    inv = ROPE_THETA ** (-jnp.arange(half, dtype=jnp.float32) * (2.0 / rot_dim))
    ang = pos.astype(jnp.float32)[:, None] * inv[None, :]
    shp = (pos.shape[0],) + (1,) * (x.ndim - 3) + (half,)
    cos = jnp.cos(ang).reshape(shp)
    sin = jnp.sin(ang).reshape(shp)
    xf = x[..., :rot_dim].astype(jnp.float32)
    x1, x2 = xf[..., :half], xf[..., half:]
    rot = jnp.concatenate([x1 * cos - x2 * sin, x2 * cos + x1 * sin], axis=-1).astype(x.dtype)
    return jnp.concatenate([rot, x[..., rot_dim:]], axis=-1)


def _query_blocks(fn, q_arrays, q_chunk):
    T = q_chunk.shape[0]
    blk = min(QBLOCK, T)
    nb = -(-T // blk)
    pad = nb * blk - T

    def to_blocks(a):
        a = jnp.pad(a, [(0, 0), (0, pad)] + [(0, 0)] * (a.ndim - 2))
        a = a.reshape((a.shape[0], nb, blk) + a.shape[2:])
        return jnp.moveaxis(a, 1, 0)

    qc = jnp.pad(q_chunk, (0, pad), mode='edge').reshape(nb, blk)
    out = lax.map(lambda args: fn(*args), tuple(to_blocks(a) for a in q_arrays) + (qc,))
    out = jnp.moveaxis(out, 0, 1)
    out = out.reshape((out.shape[0], nb * blk) + out.shape[3:])
    return out[:, :T]


def _token_mixers(h, pos, q_chunk, k_chunk, k_sel, past, w_in, a_q_norm_g, a_kv_norm_g,
                  a_w_uq, a_w_ukv, a_qk_g, b_qk_g, w_o_a, w_o_b, w_out):
    Bn, T = h.shape[0], h.shape[1]
    offs = np.cumsum(IN_SPLITS)[:-1].tolist()
    cq, ckv, kpe, bq, bk, bv, iq, iw, ik, ga, gb = jnp.split(h @ w_in, offs, axis=-1)
    cq = _rms(cq, a_q_norm_g)
    ckv = _rms(ckv, a_kv_norm_g)
    q = (cq @ a_w_uq).reshape(Bn, T, A_HEADS, A_NOPE + A_ROPE)
    qn = _rms(q[..., :A_NOPE], a_qk_g[0, :A_NOPE])
    qp = _rotary(_rms(q[..., A_NOPE:], a_qk_g[0, A_NOPE:]), pos, A_ROPE)
    kpe = _rotary(_rms(kpe, a_qk_g[1, A_NOPE:]), pos, A_ROPE)
    bq = _rotary(_rms(bq.reshape(Bn, T, B_HEADS, B_HEAD_DIM), b_qk_g[0]), pos, B_ROT)
    bk = _rotary(_rms(bk, b_qk_g[1]), pos, B_ROT)
    iq = _rotary(iq.reshape(Bn, T, IDX_HEADS, IDX_DIM), pos, IDX_ROT)
    ik = _rotary(ik, pos, IDX_ROT)
    iw = iw * IDX_W_SCALE
    new_rows = (ckv, kpe, bk, bv, ik)

    if past is None:
        lat_all, kpe_all, bk_all, bv_all, ik_all = new_rows
    else:
        lat_all, kpe_all, bk_all, bv_all, ik_all = tuple(
            jnp.concatenate([c.astype(n.dtype), n], axis=1) for c, n in zip(past, new_rows))
    S = lat_all.shape[1]
    kv = (lat_all @ a_w_ukv).reshape(Bn, S, A_HEADS, A_NOPE + A_V)
    kn = _rms(kv[..., :A_NOPE], a_qk_g[1, :A_NOPE])
    va = kv[..., A_NOPE:]

    def mla_block(qn_b, qp_b, qc):
        mask = k_chunk[None, :] <= qc[:, None]
        s = (jnp.einsum('bqhd,bshd->bhqs', qn_b, kn)
             + jnp.einsum('bqhr,bsr->bhqs', qp_b, kpe_all)).astype(jnp.float32) * A_SCALE
        p = jax.nn.softmax(jnp.where(mask, s, -jnp.inf), axis=-1).astype(va.dtype)
        return jnp.einsum('bhqs,bshd->bqhd', p, va)

    def dsa_block(bq_b, iq_b, iw_b, qc):
        blk = qc.shape[0]
        mask = k_chunk[None, :] <= qc[:, None]
        rel = jax.nn.relu(jnp.einsum('bqhd,bsd->bqhs', iq_b, ik_all).astype(jnp.float32))
        score = jnp.einsum('bqh,bqhs->bqs', iw_b.astype(jnp.float32), rel)
        _, sel = lax.top_k(jnp.where(mask, score, -jnp.inf), k_sel)
        gk = jax.vmap(lambda t, i: t[i])(bk_all, sel)
        gv = jax.vmap(lambda t, i: t[i])(bv_all, sel)
        valid = mask[jnp.arange(blk)[None, :, None], sel]
        logit = jnp.einsum('bqhd,bqkd->bqhk', bq_b, gk).astype(jnp.float32) * B_SCALE
        p = jax.nn.softmax(jnp.where(valid[:, :, None, :], logit, -jnp.inf), axis=-1).astype(gv.dtype)
        return jnp.einsum('bqhk,bqkd->bqhd', p, gv)

    o_a = _query_blocks(mla_block, (qn, qp), q_chunk).reshape(Bn, T, A_HEADS * A_V)
    o_b = _query_blocks(dsa_block, (bq, iq, iw), q_chunk).reshape(Bn, T, B_HEADS * B_HEAD_DIM)
    merged = jax.nn.sigmoid(ga) * (o_a @ w_o_a) + jax.nn.sigmoid(gb) * (o_b @ w_o_b)
    return merged @ w_out, new_rows


def _peer(h, wq, subkeys, u, v):
    shape = h.shape
    flat = h.reshape(-1, D_MODEL)
    N = flat.shape[0]
    blk = min(PEER_BLOCK, N)
    nb = -(-N // blk)
    blocks = jnp.pad(flat, ((0, nb * blk - N), (0, 0))).reshape(nb, blk, D_MODEL)
    half = PEER_DKEY // 2

    def fn(t):
        q = (t @ wq).reshape(blk, PEER_HEADS, PEER_DKEY)
        s1 = jnp.einsum('thd,hnd->thn', q[..., :half], subkeys[:, 0]).astype(jnp.float32)
        s2 = jnp.einsum('thd,hnd->thn', q[..., half:], subkeys[:, 1]).astype(jnp.float32)
        v1, i1 = lax.top_k(s1, PEER_TOPK)
        v2, i2 = lax.top_k(s2, PEER_TOPK)
        cand = (v1[..., :, None] + v2[..., None, :]).reshape(blk, PEER_HEADS, PEER_TOPK * PEER_TOPK)
        sc, pos = lax.top_k(cand, PEER_TOPK)
        e = (jnp.take_along_axis(i1, pos // PEER_TOPK, axis=-1) * N_KEYS
             + jnp.take_along_axis(i2, pos % PEER_TOPK, axis=-1))
        g = jax.nn.softmax(sc, axis=-1)
        a = jax.nn.gelu(jnp.einsum('td,thkd->thk', t, u[e]))
        return jnp.einsum('thk,thkd->td', (g * a.astype(jnp.float32)).astype(t.dtype), v[e])

    out = lax.map(fn, blocks).reshape(nb * blk, D_MODEL)[:N]
    return out.reshape(shape)


def _layer(x, pos, q_chunk, k_chunk, k_sel, past, ln1_g, w_in, a_q_norm_g, a_kv_norm_g, a_w_uq,
           a_w_ukv, a_qk_g, b_qk_g, w_o_a, w_o_b, w_out, ln2_g, peer_wq, peer_subkeys, peer_u, peer_v):
    out, rows = _token_mixers(_rms(x, ln1_g), pos, q_chunk, k_chunk, k_sel, past, w_in, a_q_norm_g,
                              a_kv_norm_g, a_w_uq, a_w_ukv, a_qk_g, b_qk_g, w_o_a, w_o_b, w_out)
    x = x + out
    x = x + _peer(_rms(x, ln2_g), peer_wq, peer_subkeys, peer_u, peer_v)
    return x, rows


def setup_inputs(seed: int = 0) -> dict:
    key = jax.random.key(seed)
    ks = jax.random.split(key, 24)
    f32 = jnp.float32
    nrm = lambda k, shp, sc: jax.random.normal(k, shp, f32) * sc
    gain = lambda k, shp: 1.0 + 0.02 * jax.random.normal(k, shp, f32)
    return {
        'x_prompt': nrm(ks[0], (BATCH, SEQ, D_MODEL), 1.0),
        'x_sample': nrm(ks[1], (DEC_BATCH, DEC_SEQ, D_MODEL), 1.0),
        'cache_a_latent': nrm(ks[2], (DEPTH, DEC_BATCH, PAST_LEN, A_KVRANK), 1.0),
        'cache_a_kpe': nrm(ks[3], (DEPTH, DEC_BATCH, PAST_LEN, A_ROPE), 1.0),
        'cache_b_k': nrm(ks[4], (DEPTH, DEC_BATCH, PAST_LEN, B_HEAD_DIM), 1.0),
        'cache_b_v': nrm(ks[5], (DEPTH, DEC_BATCH, PAST_LEN, B_HEAD_DIM), 1.0),
        'cache_b_idx_k': nrm(ks[6], (DEPTH, DEC_BATCH, PAST_LEN, IDX_DIM), 1.0),
        'meta_tokens': nrm(ks[7], (N_META, D_MODEL), 1.0),
        'ln1_g': gain(ks[8], (DEPTH, D_MODEL)),
        'w_in': nrm(ks[9], (DEPTH, D_MODEL, IN_COLS), D_MODEL ** -0.5),
        'a_q_norm_g': gain(ks[10], (DEPTH, A_QRANK)),
        'a_kv_norm_g': gain(ks[11], (DEPTH, A_KVRANK)),
        'a_w_uq': nrm(ks[12], (DEPTH, A_QRANK, A_HEADS * (A_NOPE + A_ROPE)), A_QRANK ** -0.5),
        'a_w_ukv': nrm(ks[13], (DEPTH, A_KVRANK, A_HEADS * (A_NOPE + A_V)), A_KVRANK ** -0.5),
        'a_qk_g': gain(ks[14], (DEPTH, 2, A_NOPE + A_ROPE)),
        'b_qk_g': gain(ks[15], (DEPTH, 2, B_HEAD_DIM)),
        'w_o_a': nrm(ks[16], (DEPTH, A_HEADS * A_V, D_MODEL), (A_HEADS * A_V) ** -0.5),
        'w_o_b': nrm(ks[17], (DEPTH, B_HEADS * B_HEAD_DIM, D_MODEL), (B_HEADS * B_HEAD_DIM) ** -0.5),
        'w_out': nrm(ks[18], (DEPTH, D_MODEL, D_MODEL), D_MODEL ** -0.5),
        'ln2_g': gain(ks[19], (DEPTH, D_MODEL)),
        'peer_wq': nrm(ks[20], (DEPTH, D_MODEL, PEER_HEADS * PEER_DKEY), D_MODEL ** -0.5),
        'peer_subkeys': nrm(ks[21], (DEPTH, PEER_HEADS, 2, N_KEYS, PEER_DKEY // 2), (PEER_DKEY // 2) ** -0.5),
        'peer_u': nrm(ks[22], (DEPTH, N_EXPERTS, D_MODEL), D_MODEL ** -0.5),
        'peer_v': nrm(ks[23], (DEPTH, N_EXPERTS, D_MODEL), 0.5),
    }


def reference(x_prompt, x_sample, cache_a_latent, cache_a_kpe, cache_b_k, cache_b_v, cache_b_idx_k,
              meta_tokens, ln1_g, w_in, a_q_norm_g, a_kv_norm_g, a_w_uq, a_w_ukv, a_qk_g, b_qk_g,
              w_o_a, w_o_b, w_out, ln2_g, peer_wq, peer_subkeys, peer_u, peer_v):
    Bp, Sp = x_prompt.shape[0], x_prompt.shape[1]
    Tp = N_META + Sp
    meta = jnp.broadcast_to(meta_tokens.astype(x_prompt.dtype)[None], (Bp, N_META, D_MODEL))
    xp = jnp.concatenate([meta, x_prompt], axis=1)
    pos_p = jnp.arange(Tp, dtype=jnp.int32)
    chunk_p = jnp.where(pos_p < N_META, -1, (pos_p - N_META) // CHUNK)
    k_sel_p = min(IDX_TOPK, Sp // 4)
    Ts = x_sample.shape[1]
    P = cache_a_latent.shape[2]
    xs = x_sample
    pos_s = P + jnp.arange(Ts, dtype=jnp.int32)
    q_chunk_s = jnp.ones((Ts,), jnp.int32)
    k_chunk_s = jnp.concatenate([jnp.zeros((P,), jnp.int32), jnp.ones((Ts,), jnp.int32)])
    k_sel_s = min(IDX_TOPK, (P + Ts) // 4)

    rows_p, rows_s = [], []
    for l in range(DEPTH):
        lw = (ln1_g[l], w_in[l], a_q_norm_g[l], a_kv_norm_g[l], a_w_uq[l], a_w_ukv[l], a_qk_g[l],
              b_qk_g[l], w_o_a[l], w_o_b[l], w_out[l], ln2_g[l], peer_wq[l], peer_subkeys[l],
              peer_u[l], peer_v[l])
        xp, rp = _layer(xp, pos_p, chunk_p, chunk_p, k_sel_p, None, *lw)
        past = (cache_a_latent[l], cache_a_kpe[l], cache_b_k[l], cache_b_v[l], cache_b_idx_k[l])
        xs, rs = _layer(xs, pos_s, q_chunk_s, k_chunk_s, k_sel_s, past, *lw)
        rows_p.append(rp)
        rows_s.append(rs)

    new_a_latent_p = jnp.stack([r[0] for r in rows_p])
    new_a_kpe_p = jnp.stack([r[1] for r in rows_p])
    new_b_k_p = jnp.stack([r[2] for r in rows_p])
    new_b_v_p = jnp.stack([r[3] for r in rows_p])
    new_b_idx_k_p = jnp.stack([r[4] for r in rows_p])
    new_a_latent_s = jnp.stack([r[0] for r in rows_s])
    new_a_kpe_s = jnp.stack([r[1] for r in rows_s])
    new_b_k_s = jnp.stack([r[2] for r in rows_s])
    new_b_v_s = jnp.stack([r[3] for r in rows_s])
    new_b_idx_k_s = jnp.stack([r[4] for r in rows_s])
    y_prompt = xp[:, N_META:]
    y_sample = xs
    return (y_prompt, y_sample, new_a_latent_p, new_a_kpe_p, new_b_k_p, new_b_v_p, new_b_idx_k_p,
            new_a_latent_s, new_a_kpe_s, new_b_k_s, new_b_v_s, new_b_idx_k_s)
```

```python
import functools

import jax
import jax.numpy as jnp
import numpy as np
from jax import lax
from jax.experimental import pallas as pl
from jax.experimental.pallas import tpu as pltpu

F32 = jnp.float32
BF16 = jnp.bfloat16

LANES = 128
VMEM_LIMIT = 52 * 1024 * 1024

CHUNK = 64
N_META = 16
ROPE_THETA = 500000.0
EPS = 1e-6

A_HEADS = 8
A_NOPE = 64
A_ROPE = 32
A_V = 64
A_QRANK = 256
A_KVRANK = 128
A_SCALE = (A_NOPE + A_ROPE) ** -0.5

B_HEADS = 8
B_HEAD_DIM = 64
B_ROT = B_HEAD_DIM // 4
B_SCALE = B_HEAD_DIM ** -0.5
IDX_HEADS = 8
IDX_DIM = 64
IDX_ROT = IDX_DIM // 4
IDX_TOPK = 256
IDX_W_SCALE = (IDX_HEADS * IDX_DIM) ** -0.5

PEER_HEADS = 8
PEER_DKEY = 128
N_KEYS = 128
PEER_TOPK = 16

IN_SPLITS = (A_QRANK, A_KVRANK, A_ROPE, B_HEADS * B_HEAD_DIM, B_HEAD_DIM, B_HEAD_DIM,
             IDX_HEADS * IDX_DIM, IDX_HEADS, IDX_DIM)

NEG_BIG = -1e30
INT_MIN = -2147483648
PAD_CHUNK = 1 << 30

NT_DIMS = (((1,), (1,)), ((), ()))


def _cparams(*sem):
    return pltpu.CompilerParams(dimension_semantics=sem, vmem_limit_bytes=VMEM_LIMIT)


def _round_up(n, m):
    return -(-n // m) * m


def _pick_tile(n, target):
    best = LANES
    for t in range(LANES, min(n, target) + 1, LANES):
        if n % t == 0:
            best = t
    return best


def _rot_tables(pos, rot_dim, offsets):
    half = rot_dim // 2
    inv = ROPE_THETA ** (-jnp.arange(half, dtype=F32) * (2.0 / rot_dim))
    ang = pos.astype(F32)[:, None] * inv[None, :]
    cos, sin = jnp.cos(ang), jnp.sin(ang)
    t = pos.shape[0]
    c = jnp.ones((t, LANES), F32)
    sm = jnp.zeros((t, LANES), F32)
    sp = jnp.zeros((t, LANES), F32)
    for o in offsets:
        c = c.at[:, o:o + half].set(cos).at[:, o + half:o + rot_dim].set(cos)
        sm = sm.at[:, o:o + half].set(-sin)
        sp = sp.at[:, o + half:o + rot_dim].set(sin)
    return jnp.stack([c, sm, sp])


def _all_tables(pos):
    return jnp.concatenate([
        _rot_tables(pos, A_ROPE, (A_NOPE,)),
        _rot_tables(pos, B_ROT, (0, B_HEAD_DIM)),
        _rot_tables(pos, B_ROT, (0,)),
    ])


def _rot(x, tab_ref, base, half):
    c = tab_ref[base]
    sm = tab_ref[base + 1]
    sp = tab_ref[base + 2]
    return x * c + pltpu.roll(x, LANES - half, 1) * sm + pltpu.roll(x, half, 1) * sp


SLAB_CQ, SLAB_CKV, SLAB_KPE, SLAB_BQ, SLAB_BK, SLAB_BV = 0, 2, 3, 4, 8, 9
SLAB_IQ, SLAB_IW, SLAB_IK, SLAB_GA, N_SLAB_SMALL = 10, 14, 15, 16, 16


def _cols(s0, n=1):
    return slice(s0 * LANES, (s0 + n) * LANES)


def _inproj_kernel(x_ref, ln1_ref, win_ref, wg_ref, gq_ref, gkv_ref, wuq_ref, gs_ref, tab_ref,
                   lat_ref, kpes_ref, bk_ref, bv_ref, ik_ref, qa_ref, qb_ref, iq_ref, iw_ref,
                   ga_ref, gb_ref):
    d = x_ref.shape[1]
    x = x_ref[...]
    h = x * lax.rsqrt(jnp.mean(x * x, axis=-1, keepdims=True) + EPS) * ln1_ref[...]
    hb = h.astype(BF16)
    lane = lax.broadcasted_iota(jnp.int32, (1, LANES), 1)
    lo = lane < B_HEAD_DIM

    def proj(s0, n=1):
        return jnp.dot(hb, win_ref[:, _cols(s0, n)], preferred_element_type=F32)

    def seg_rs(x2, mask, n):
        return lax.rsqrt(jnp.sum(jnp.where(mask, x2, 0.0), axis=-1, keepdims=True) * (1.0 / n) + EPS)

    cq = proj(SLAB_CQ, 2)
    cq = cq * lax.rsqrt(jnp.mean(cq * cq, axis=-1, keepdims=True) + EPS) * gq_ref[...]
    q = jnp.dot(cq.astype(BF16), wuq_ref[...], preferred_element_type=F32)
    rope_m = jnp.logical_and(lane >= A_NOPE, lane < A_NOPE + A_ROPE)
    for hd in range(A_HEADS):
        s = q[:, _cols(hd)]
        s2 = s * s
        sc = jnp.where(lo, seg_rs(s2, lo, A_NOPE), seg_rs(s2, rope_m, A_ROPE))
        qa_ref[:, _cols(hd)] = _rot(s * sc * gs_ref[0:1, :], tab_ref, 0, A_ROPE // 2).astype(BF16)

    ckv = proj(SLAB_CKV)
    lat_ref[...] = ckv * lax.rsqrt(jnp.mean(ckv * ckv, axis=-1, keepdims=True) + EPS) * gkv_ref[...]
    kp = proj(SLAB_KPE)
    kp = kp * seg_rs(kp * kp, rope_m, A_ROPE) * gs_ref[1:2, :]
    kpes_ref[...] = _rot(kp, tab_ref, 0, A_ROPE // 2)

    for j in range(B_HEADS // 2):
        s = proj(SLAB_BQ + j)
        s2 = s * s
        sc = jnp.where(lo, seg_rs(s2, lo, B_HEAD_DIM), seg_rs(s2, jnp.logical_not(lo), B_HEAD_DIM))
        qb_ref[:, _cols(j)] = _rot(s * sc * gs_ref[2:3, :], tab_ref, 3, B_ROT // 2).astype(BF16)
    s = proj(SLAB_BK)
    s = s * seg_rs(s * s, lo, B_HEAD_DIM) * gs_ref[3:4, :]
    bk_ref[...] = _rot(s, tab_ref, 6, B_ROT // 2)[:, :B_HEAD_DIM]
    bv_ref[...] = proj(SLAB_BV)[:, :B_HEAD_DIM]

    for j in range(IDX_HEADS // 2):
        iq_ref[:, _cols(j)] = _rot(proj(SLAB_IQ + j), tab_ref, 3, IDX_ROT // 2).astype(BF16)
    iw_ref[...] = proj(SLAB_IW) * IDX_W_SCALE
    ik_ref[...] = _rot(proj(SLAB_IK), tab_ref, 6, IDX_ROT // 2)[:, :IDX_DIM]

    nd = d // LANES
    for j in range(nd):
        ga_ref[:, _cols(j)] = jax.nn.sigmoid(
            jnp.dot(hb, wg_ref[:, _cols(j)], preferred_element_type=F32))
        gb_ref[:, _cols(j)] = jax.nn.sigmoid(
            jnp.dot(hb, wg_ref[:, _cols(nd + j)], preferred_element_type=F32))


def _in_proj(x, tabs, reps, tm, w):
    n, d = x.shape
    r = tabs.shape[1]
    npb = r // tm
    assert r % tm == 0 and n == reps * r
    tok = lambda width: pl.BlockSpec((tm, width), lambda p, b: (b * npb + p, 0))
    full = lambda a: pl.BlockSpec(a.shape, lambda p, b: (0,) * a.ndim)
    outs = [
        ((n, A_KVRANK), F32), ((n, LANES), F32), ((n, B_HEAD_DIM), F32), ((n, B_HEAD_DIM), F32),
        ((n, IDX_DIM), F32), ((n, A_HEADS * LANES), BF16), ((n, B_HEADS * B_HEAD_DIM), BF16),
        ((n, IDX_HEADS * IDX_DIM), BF16), ((n, LANES), F32), ((n, d), F32), ((n, d), F32),
    ]
    return pl.pallas_call(
        _inproj_kernel,
        grid=(npb, reps),
        in_specs=[tok(d), full(w['ln1']), full(w['win']), full(w['wg']), full(w['gq']), full(w['gkv']),
                  full(w['wuq']), full(w['gs']),
                  pl.BlockSpec((tabs.shape[0], tm, LANES), lambda p, b: (0, p, 0))],
        out_specs=[tok(s[1]) for s, _ in outs],
        out_shape=[jax.ShapeDtypeStruct(s, t) for s, t in outs],
        compiler_params=_cparams("arbitrary", "arbitrary"),
        name="in_proj",
    )(x, w['ln1'], w['win'], w['wg'], w['gq'], w['gkv'], w['wuq'], w['gs'], tabs)


def _kvprep_kernel(lat_ref, kpes_ref, wk_ref, wv_ref, gk_ref, ka_ref, va_ref):
    lat = lat_ref[...].astype(BF16)
    kk = jnp.dot(lat, wk_ref[...], preferred_element_type=F32)
    vv = jnp.dot(lat, wv_ref[...], preferred_element_type=F32)
    kpes = kpes_ref[...]
    for hd in range(A_HEADS):
        s = kk[:, _cols(hd)]
        rs = lax.rsqrt(jnp.sum(s * s, axis=-1, keepdims=True) * (1.0 / A_NOPE) + EPS)
        ka_ref[hd] = (s * rs * gk_ref[...] + kpes).astype(BF16)
        va_ref[hd] = vv[:, _cols(hd)].astype(BF16)


def _kv_prep(lat_keys, kpes_keys, w):
    b, s, _ = lat_keys.shape
    ts = _pick_tile(s, 512)
    row = pl.BlockSpec((None, ts, LANES), lambda i, j: (i, j, 0))
    full = lambda a: pl.BlockSpec(a.shape, lambda i, j: (0,) * a.ndim)
    hd = pl.BlockSpec((None, A_HEADS, ts, LANES), lambda i, j: (i, 0, j, 0))
    shp = jax.ShapeDtypeStruct((b, A_HEADS, s, LANES), BF16)
    return pl.pallas_call(
        _kvprep_kernel,
        grid=(b, s // ts),
        in_specs=[row, row, full(w['wk']), full(w['wv']), full(w['gk'])],
        out_specs=[hd, hd],
        out_shape=[shp, shp],
        compiler_params=_cparams("arbitrary", "arbitrary"),
        name="kv_prep",
    )(lat_keys, kpes_keys, w['wk'], w['wv'], w['gk'])


def _mla_kernel(q_ref, k_ref, v_ref, qc_ref, kc_ref, o_ref):
    s = lax.dot_general(q_ref[...], k_ref[...], NT_DIMS, preferred_element_type=F32) * A_SCALE
    s = jnp.where(kc_ref[...] <= qc_ref[...], s, NEG_BIG)
    p = jnp.exp(s - jnp.max(s, axis=-1, keepdims=True))
    p = p * (1.0 / jnp.sum(p, axis=-1, keepdims=True))
    o_ref[...] = jnp.dot(p.astype(BF16), v_ref[...], preferred_element_type=F32).astype(BF16)


def _mla_attn(qa, ka, va, qc, kc, tq):
    n = qa.shape[0]
    b, _, s, _ = ka.shape
    nq = n // (b * tq)
    qspec = pl.BlockSpec((tq, LANES), lambda i, h, j: (i * nq + j, h))
    kspec = pl.BlockSpec((None, None, s, LANES), lambda i, h, j: (i, h, 0, 0))
    return pl.pallas_call(
        _mla_kernel,
        grid=(b, A_HEADS, nq),
        in_specs=[qspec, kspec, kspec,
                  pl.BlockSpec((tq, 1), lambda i, h, j: (j, 0)),
                  pl.BlockSpec((1, s), lambda i, h, j: (0, 0))],
        out_specs=qspec,
        out_shape=jax.ShapeDtypeStruct((n, A_HEADS * LANES), BF16),
        compiler_params=_cparams("arbitrary", "arbitrary", "arbitrary"),
        name="mla_attn",
    )(qa, ka, va, qc, kc)


def _dsa_kernel(iq_ref, iw_ref, qb_ref, ik2_ref, bk2_ref, bv2_ref, qc_ref, kc_ref, kidx_ref,
                o_ref, key_s, bias_s, *, k_sel, idx_bits):
    tq = key_s.shape[0]
    adm = kc_ref[...] <= qc_ref[...]
    iw = iw_ref[...]
    score = None
    for j in range(IDX_HEADS // 2):
        iqj = iq_ref[:, _cols(j)]
        for half in range(2):
            hd = 2 * j + half
            rel = jnp.maximum(
                lax.dot_general(iqj, ik2_ref[half], NT_DIMS, preferred_element_type=F32), 0.0)
            term = iw[:, hd:hd + 1] * rel
            score = term if score is None else score + term

    bits = lax.bitcast_convert_type(score, jnp.int32)
    key = jnp.where(bits < 0, bits ^ jnp.int32(0x7FFFFFFF), bits)
    key = jnp.where(key == -1, 0, key)
    key_s[...] = jnp.where(adm, key, INT_MIN)

    kf = float(k_sel)

    def count(mask):
        return jnp.sum(jnp.where(mask, 1.0, 0.0), axis=-1, keepdims=True)

    def value_step(i, cur):
        cand = cur + lax.shift_left(jnp.int32(1), 31 - i)
        return jnp.where(count(key_s[...] >= cand) >= kf, cand, cur)

    tau = lax.fori_loop(0, 32, value_step, jnp.full((tq, 1), INT_MIN, jnp.int32))
    key = key_s[...]
    ge = key >= tau
    c_ge = count(ge)
    bias_s[...] = jnp.where(jnp.logical_and(ge, adm), 0.0, NEG_BIG)
    amb = jnp.logical_and(tau > INT_MIN, c_ge > kf)

    @pl.when(jnp.max(jnp.where(amb, 1.0, 0.0)) > 0.0)
    def _():
        key = key_s[...]
        gt = key > tau
        tie = key == tau
        need = kf - count(gt)
        kidx = kidx_ref[...]

        def index_step(i, cur):
            cand = cur + lax.shift_left(jnp.int32(1), idx_bits - 1 - i)
            below = count(jnp.logical_and(tie, kidx < cand))
            return jnp.where(below < need, cand, cur)

        last = lax.fori_loop(0, idx_bits, index_step, jnp.zeros((tq, 1), jnp.int32))
        sel = jnp.logical_or(gt, jnp.logical_and(tie, kidx <= last))
        bias_s[...] = jnp.where(jnp.logical_and(sel, adm), 0.0, NEG_BIG)

    for j in range(B_HEADS // 2):
        qbj = qb_ref[:, _cols(j)]
        acc = None
        for half in range(2):
            lg = lax.dot_general(qbj, bk2_ref[half], NT_DIMS, preferred_element_type=F32)
            lg = lg * B_SCALE + bias_s[...]
            p = jnp.exp(lg - jnp.max(lg, axis=-1, keepdims=True))
            p = p * (1.0 / jnp.sum(p, axis=-1, keepdims=True))
            o = jnp.dot(p.astype(BF16), bv2_ref[half], preferred_element_type=F32)
            acc = o if acc is None else acc + o
        o_ref[:, _cols(j)] = acc.astype(BF16)


def _dsa_attn(iq, iw, qb, ik2, bk2, bv2, qc, kc, kidx, tq, k_sel):
    n = iq.shape[0]
    b, _, s, _ = ik2.shape
    nq = n // (b * tq)
    idx_bits = max(1, int(np.ceil(np.log2(s + N_META + 1))))
    tok = lambda width: pl.BlockSpec((tq, width), lambda i, j: (i * nq + j, 0))
    kspec = pl.BlockSpec((None, 2, s, LANES), lambda i, j: (i, 0, 0, 0))
    row = pl.BlockSpec((1, s), lambda i, j: (0, 0))
    return pl.pallas_call(
        functools.partial(_dsa_kernel, k_sel=k_sel, idx_bits=idx_bits),
        grid=(b, nq),
        in_specs=[tok(iq.shape[1]), tok(LANES), tok(qb.shape[1]), kspec, kspec, kspec,
                  pl.BlockSpec((tq, 1), lambda i, j: (j, 0)), row, row],
        out_specs=tok(qb.shape[1]),
        out_shape=jax.ShapeDtypeStruct(qb.shape, BF16),
        scratch_shapes=[pltpu.VMEM((tq, s), jnp.int32), pltpu.VMEM((tq, s), F32)],
        compiler_params=_cparams("arbitrary", "arbitrary"),
        name="dsa_attn",
    )(iq, iw, qb, ik2, bk2, bv2, qc, kc, kidx)


def _outproj_kernel(x_ref, oa_ref, ob_ref, ga_ref, gb_ref, woa_ref, wob_ref, wout_ref, ln2_ref,
                    y_ref, t_ref):
    ya = jnp.dot(oa_ref[...], woa_ref[...], preferred_element_type=F32)
    yb = jnp.dot(ob_ref[...], wob_ref[...], preferred_element_type=F32)
    mg = ga_ref[...] * ya + gb_ref[...] * yb
    y = x_ref[...] + jnp.dot(mg.astype(BF16), wout_ref[...], preferred_element_type=F32)
    y_ref[...] = y
    t = y * lax.rsqrt(jnp.mean(y * y, axis=-1, keepdims=True) + EPS) * ln2_ref[...]
    t_ref[...] = t.astype(BF16)


def _out_proj(x, oa, ob, ga, gb, w, tm):
    n, d = x.shape
    tok = lambda a: pl.BlockSpec((tm, a.shape[1]), lambda i: (i, 0))
    full = lambda a: pl.BlockSpec(a.shape, lambda i: (0,) * a.ndim)
    return pl.pallas_call(
        _outproj_kernel,
        grid=(n // tm,),
        in_specs=[tok(x), tok(oa), tok(ob), tok(ga), tok(gb),
                  full(w['woa']), full(w['wob']), full(w['wout']), full(w['ln2'])],
        out_specs=[tok(x), tok(x)],
        out_shape=[jax.ShapeDtypeStruct((n, d), F32), jax.ShapeDtypeStruct((n, d), BF16)],
        compiler_params=_cparams("arbitrary"),
        name="out_proj",
    )(x, oa, ob, ga, gb, w['woa'], w['wob'], w['wout'], w['ln2'])


def _top_values(s, k):
    out = []
    for _ in range(k):
        m = jnp.max(s, axis=0, keepdims=True)
        out.append(m)
        s = jnp.where(s == m, -jnp.inf, s)
    return out


def _stack_rows(rows):
    n = len(rows)
    it = lax.broadcasted_iota(jnp.int32, (n, rows[0].shape[1]), 0)
    acc = jnp.broadcast_to(rows[0], it.shape)
    for i in range(1, n):
        acc = jnp.where(it == i, rows[i], acc)
    return acc


def _kth_pair_sums(top1, top2, k):
    b_st = _stack_rows(top2)
    row = lax.broadcasted_iota(jnp.int32, (8, b_st.shape[1]), 0)
    blocks = [top1[0] + b_st]
    i = 1
    while k // (i + 1) >= 2:
        n_i = k // (i + 1)
        blk = top1[i] + b_st[:8]
        blocks.append(jnp.where(row < n_i, blk, -jnp.inf))
        i += 1
    tail = _stack_rows(top1[i:] + [top1[-1]] * (8 - (k - i)))
    blocks.append(jnp.where(row < k - i, tail + top2[0], -jnp.inf))
    return _top_values(jnp.concatenate(blocks, axis=0), k)


def _peer_kernel(t_ref, y_ref, wqt_ref, sk1_ref, sk2_ref, u_ref, vt_ref, o_ref,
                 s1_s, s2_s, a_s, b_s, tau_s, gw_s, acc_s):
    c = pl.program_id(1)
    te = u_ref.shape[0]
    half = PEER_DKEY // 2

    @pl.when(c == 0)
    def _route():
        qt = lax.dot_general(wqt_ref[...], t_ref[...], NT_DIMS, preferred_element_type=F32)
        qt = qt.astype(BF16)
        for hd in range(PEER_HEADS):
            q1 = qt[hd * PEER_DKEY:hd * PEER_DKEY + half]
            q2 = qt[hd * PEER_DKEY + half:(hd + 1) * PEER_DKEY]
            s1 = jnp.dot(sk1_ref[hd], q1, preferred_element_type=F32)
            s2 = jnp.dot(sk2_ref[hd], q2, preferred_element_type=F32)
            top1 = _top_values(s1, PEER_TOPK)
            top2 = _top_values(s2, PEER_TOPK)
            top = _kth_pair_sums(top1, top2, PEER_TOPK)
            z = None
            for tk in top:
                e = jnp.exp(tk - top[0])
                z = e if z is None else z + e
            s1_s[hd] = s1
            s2_s[hd] = s2
            a_s[hd] = jnp.exp(s1 - top1[0])
            b_s[hd] = jnp.exp(s2 - top2[0]) * (1.0 / z)
            tau_s[pl.ds(hd, 1), :] = top[PEER_TOPK - 1]
        acc_s[...] = jnp.zeros_like(acc_s)

    at = lax.dot_general(u_ref[...], t_ref[...], NT_DIMS, preferred_element_type=F32)
    for k in range(te // N_KEYS):
        i1 = c * (te // N_KEYS) + k
        w = None
        for hd in range(PEER_HEADS):
            sm = s2_s[hd] + s1_s[hd, pl.ds(i1, 1), :]
            term = jnp.where(sm >= tau_s[pl.ds(hd, 1), :], b_s[hd] * a_s[hd, pl.ds(i1, 1), :], 0.0)
            w = term if w is None else w + term
        g = jax.nn.gelu(at[k * N_KEYS:(k + 1) * N_KEYS])
        gw_s[pl.ds(k * N_KEYS, N_KEYS), :] = (w * g).astype(BF16)
    acc_s[...] += jnp.dot(vt_ref[...], gw_s[...], preferred_element_type=F32)

    @pl.when(c == pl.num_programs(1) - 1)
    def _finish():
        o_ref[...] = y_ref[...] + acc_s[...].T


def _peer(t, y, w, tm, te):
    n, d = y.shape
    ne = w['u'].shape[0]
    full = lambda a: pl.BlockSpec(a.shape, lambda i, c: (0,) * a.ndim)
    tok = pl.BlockSpec((tm, d), lambda i, c: (i, 0))
    hk = (PEER_HEADS, N_KEYS, tm)
    return pl.pallas_call(
        _peer_kernel,
        grid=(n // tm, ne // te),
        in_specs=[tok, tok, full(w['wqt']), full(w['sk1']), full(w['sk2']),
                  pl.BlockSpec((te, d), lambda i, c: (c, 0)),
                  pl.BlockSpec((d, te), lambda i, c: (0, c))],
        out_specs=tok,
        out_shape=jax.ShapeDtypeStruct((n, d), F32),
        scratch_shapes=[pltpu.VMEM(hk, F32), pltpu.VMEM(hk, F32), pltpu.VMEM(hk, F32),
                        pltpu.VMEM(hk, F32), pltpu.VMEM((PEER_HEADS, tm), F32),
                        pltpu.VMEM((te, tm), BF16), pltpu.VMEM((d, tm), F32)],
        compiler_params=_cparams("arbitrary", "arbitrary"),
        name="peer",
    )(t, y, w['wqt'], w['sk1'], w['sk2'], w['u'], w['vt'])


def _prep_weights(ln1_g, w_in, a_q_norm_g, a_kv_norm_g, a_w_uq, a_w_ukv, a_qk_g, b_qk_g,
                  w_o_a, w_o_b, w_out, ln2_g, peer_wq, peer_subkeys, peer_u, peer_v):
    d = w_in.shape[0]
    offs = np.cumsum(IN_SPLITS).tolist()
    cq, ckv, kpe, bq, bk, bv, iq, iw, ik, gates = jnp.split(w_in, offs, axis=1)
    z = lambda n: jnp.zeros((d, n), w_in.dtype)
    win = jnp.concatenate([
        cq, ckv, z(A_NOPE), kpe, z(LANES - A_NOPE - A_ROPE), bq,
        bk, z(LANES - B_HEAD_DIM), bv, z(LANES - B_HEAD_DIM), iq,
        iw, z(LANES - IDX_HEADS), ik, z(LANES - IDX_DIM)], axis=1).astype(BF16)
    assert win.shape[1] == N_SLAB_SMALL * LANES

    def pad_last(a, n):
        return jnp.pad(a, [(0, 0)] * (a.ndim - 1) + [(0, n - a.shape[-1])])

    wuq = pad_last(a_w_uq.reshape(A_QRANK, A_HEADS, A_NOPE + A_ROPE), LANES)
    wuq = wuq.reshape(A_QRANK, A_HEADS * LANES).astype(BF16)
    ukv = a_w_ukv.reshape(A_KVRANK, A_HEADS, A_NOPE + A_V)
    wk = pad_last(ukv[..., :A_NOPE], LANES).reshape(A_KVRANK, A_HEADS * LANES).astype(BF16)
    wv = pad_last(ukv[..., A_NOPE:], LANES).reshape(A_KVRANK, A_HEADS * LANES).astype(BF16)
    woa = pad_last(w_o_a.reshape(A_HEADS, A_V, d).transpose(0, 2, 1), LANES)
    woa = woa.transpose(0, 2, 1).reshape(A_HEADS * LANES, d).astype(BF16)

    row = lambda v: pad_last(v, LANES)[None, :]
    gs = jnp.concatenate([
        row(a_qk_g[0]),
        row(jnp.concatenate([jnp.zeros((A_NOPE,), F32), a_qk_g[1, A_NOPE:]])),
        row(jnp.concatenate([b_qk_g[0], b_qk_g[0]])),
        row(b_qk_g[1]),
        jnp.zeros((4, LANES), F32)], axis=0)
    return dict(
        ln1=ln1_g[None, :], win=win, wg=gates.astype(BF16), gq=a_q_norm_g[None, :],
        gkv=a_kv_norm_g[None, :], wuq=wuq, gs=gs, wk=wk, wv=wv, gk=row(a_qk_g[1, :A_NOPE]),
        woa=woa, wob=w_o_b.astype(BF16), wout=w_out.astype(BF16), ln2=ln2_g[None, :],
        wqt=peer_wq.T.astype(BF16),
        sk1=peer_subkeys[:, 0].astype(BF16), sk2=peer_subkeys[:, 1].astype(BF16),
        u=peer_u.astype(BF16), vt=peer_v.T.astype(BF16))


def _key_layout(rows, extra, s_pad):
    k = jnp.concatenate([rows, extra], axis=1)
    return jnp.pad(k, ((0, 0), (0, s_pad - k.shape[1]), (0, 0)))


def _pair_slabs(k):
    kb = k.astype(BF16)
    z = jnp.zeros_like(kb)
    return jnp.stack([jnp.concatenate([kb, z], -1), jnp.concatenate([z, kb], -1)], axis=1)


def _mixers_and_peer(x, rows, keys, qc, kc, kidx, k_sel, w, tq, tm, te):
    lat_k, kpes_k, bk_k, bv_k, ik_k = keys
    (_, _, _, _, _, qa, qb, iq, iw, ga, gb) = rows
    ka, va = _kv_prep(lat_k, kpes_k, w)
    oa = _mla_attn(qa, ka, va, qc, kc, tq)
    ob = _dsa_attn(iq, iw, qb, _pair_slabs(ik_k), _pair_slabs(bk_k), _pair_slabs(bv_k),
                   qc, kc, kidx, tq, k_sel)
    y1, t2 = _out_proj(x, oa, ob, ga, gb, w, tm)
    return _peer(t2, y1, w, tm, te)


def kernel(x_prompt, x_sample, cache_a_latent, cache_a_kpe, cache_b_k, cache_b_v, cache_b_idx_k,
           meta_tokens, ln1_g, w_in, a_q_norm_g, a_kv_norm_g, a_w_uq, a_w_ukv, a_qk_g, b_qk_g,
           w_o_a, w_o_b, w_out, ln2_g, peer_wq, peer_subkeys, peer_u, peer_v):
    bp, sp, d = x_prompt.shape
    bs, ts, _ = x_sample.shape
    depth, _, past, _ = cache_a_latent.shape
    assert depth == 1 and sp % CHUNK == 0
    w = _prep_weights(ln1_g[0], w_in[0], a_q_norm_g[0], a_kv_norm_g[0], a_w_uq[0], a_w_ukv[0],
                      a_qk_g[0], b_qk_g[0], w_o_a[0], w_o_b[0], w_out[0], ln2_g[0], peer_wq[0],
                      peer_subkeys[0], peer_u[0], peer_v[0])
    tm = 256
    te = 512
    i32 = jnp.int32

    xq = x_prompt.reshape(bp * sp, d)
    rows_p = _in_proj(xq, _all_tables(N_META + jnp.arange(sp, dtype=i32)), bp, tm, w)
    rows_m = _in_proj(meta_tokens.astype(F32), _all_tables(jnp.arange(N_META, dtype=i32)), 1, N_META, w)
    xs = x_sample.reshape(bs * ts, d)
    pos_s = past + jnp.arange(ts, dtype=i32)
    rows_s = _in_proj(xs, _all_tables(jnp.tile(pos_s, bs)), 1, bs * ts, w)

    s_p = _round_up(sp + N_META, LANES)
    per_b = lambda a, b, t: a.reshape(b, t, a.shape[-1])
    meta_b = lambda a: jnp.broadcast_to(a[None], (bp,) + a.shape)
    keys_p = tuple(_key_layout(per_b(r, bp, sp), meta_b(m), s_p)
                   for r, m in zip(rows_p[:5], rows_m[:5]))
    chunk_q = jnp.arange(sp, dtype=i32) // CHUNK
    kc_p = jnp.concatenate([chunk_q, jnp.full((N_META,), -1, i32),
                            jnp.full((s_p - sp - N_META,), PAD_CHUNK, i32)])[None, :]
    kidx_p = jnp.concatenate([N_META + jnp.arange(sp, dtype=i32), jnp.arange(N_META, dtype=i32),
                              sp + N_META + jnp.arange(s_p - sp - N_META, dtype=i32)])[None, :]
    y_p = _mixers_and_peer(xq, rows_p, keys_p, chunk_q[:, None], kc_p, kidx_p,
                           min(IDX_TOPK, sp // 4), w, 256, tm, te)

    s_s = _round_up(past + ts, LANES)
    kpes_cache = jnp.pad(cache_a_kpe[0], ((0, 0), (0, 0), (A_NOPE, LANES - A_NOPE - A_ROPE)))
    caches = (cache_a_latent[0], kpes_cache, cache_b_k[0], cache_b_v[0], cache_b_idx_k[0])
    keys_s = tuple(jnp.pad(jnp.concatenate([c.astype(F32), per_b(r, bs, ts)], axis=1),
                           ((0, 0), (0, s_s - past - ts), (0, 0)))
                   for c, r in zip(caches, rows_s[:5]))
    kc_s = jnp.concatenate([jnp.zeros((past,), i32), jnp.ones((ts,), i32),
                            jnp.full((s_s - past - ts,), PAD_CHUNK, i32)])[None, :]
    kidx_s = jnp.arange(s_s, dtype=i32)[None, :]
    y_s = _mixers_and_peer(xs, rows_s, keys_s, jnp.ones((ts, 1), i32), kc_s, kidx_s,
                           min(IDX_TOPK, (past + ts) // 4), w, ts, bs * ts, te)

    kpe_of = lambda r: r[:, A_NOPE:A_NOPE + A_ROPE]

    def new_rows_p(r, m):
        return jnp.concatenate([meta_b(m), per_b(r, bp, sp)], axis=1)[None]

    outs_p = [new_rows_p(rows_p[0], rows_m[0]), new_rows_p(kpe_of(rows_p[1]), kpe_of(rows_m[1]))]
    outs_p += [new_rows_p(rows_p[i], rows_m[i]) for i in (2, 3, 4)]
    outs_s = [per_b(rows_s[0], bs, ts)[None], per_b(kpe_of(rows_s[1]), bs, ts)[None]]
    outs_s += [per_b(rows_s[i], bs, ts)[None] for i in (2, 3, 4)]
    return (y_p.reshape(bp, sp, d), y_s.reshape(bs, ts, d), *outs_p, *outs_s)
```

```python
import functools

import jax
import jax.numpy as jnp
import numpy as np
from jax import lax
from jax.experimental import pallas as pl
from jax.experimental.pallas import tpu as pltpu

F32 = jnp.float32
BF16 = jnp.bfloat16

LANES = 128
VMEM_LIMIT = 52 * 1024 * 1024

CHUNK = 64
N_META = 16
ROPE_THETA = 500000.0
EPS = 1e-6

A_HEADS = 8
A_NOPE = 64
A_ROPE = 32
A_V = 64
A_QRANK = 256
A_KVRANK = 128
A_SCALE = (A_NOPE + A_ROPE) ** -0.5

B_HEADS = 8
B_HEAD_DIM = 64
B_ROT = B_HEAD_DIM // 4
B_SCALE = B_HEAD_DIM ** -0.5
IDX_HEADS = 8
IDX_DIM = 64
IDX_ROT = IDX_DIM // 4
IDX_TOPK = 256
IDX_W_SCALE = (IDX_HEADS * IDX_DIM) ** -0.5

PEER_HEADS = 8
PEER_DKEY = 128
N_KEYS = 128
PEER_TOPK = 16
PEER_SUB = 512
PROMPT_GROUPS = 8

IN_SPLITS = (A_QRANK, A_KVRANK, A_ROPE, B_HEADS * B_HEAD_DIM, B_HEAD_DIM, B_HEAD_DIM,
             IDX_HEADS * IDX_DIM, IDX_HEADS, IDX_DIM)

NEG_BIG = -1e30
INT_MIN = -2147483648
PAD_CHUNK = 1 << 30

NT_DIMS = (((1,), (1,)), ((), ()))


def _cparams(*sem):
    return pltpu.CompilerParams(dimension_semantics=sem, vmem_limit_bytes=VMEM_LIMIT)


def _round_up(n, m):
    return -(-n // m) * m


def _pick_tile(n, target):
    best = LANES
    for t in range(LANES, min(n, target) + 1, LANES):
        if n % t == 0:
            best = t
    return best


def _rot_tables(pos, rot_dim, offsets):
    half = rot_dim // 2
    inv = ROPE_THETA ** (-jnp.arange(half, dtype=F32) * (2.0 / rot_dim))
    ang = pos.astype(F32)[:, None] * inv[None, :]
    cos, sin = jnp.cos(ang), jnp.sin(ang)
    t = pos.shape[0]
    c = jnp.ones((t, LANES), F32)
    sm = jnp.zeros((t, LANES), F32)
    sp = jnp.zeros((t, LANES), F32)
    for o in offsets:
        c = c.at[:, o:o + half].set(cos).at[:, o + half:o + rot_dim].set(cos)
        sm = sm.at[:, o:o + half].set(-sin)
        sp = sp.at[:, o + half:o + rot_dim].set(sin)
    return jnp.stack([c, sm, sp])


def _all_tables(pos):
    return jnp.concatenate([
        _rot_tables(pos, A_ROPE, (A_NOPE,)),
        _rot_tables(pos, B_ROT, (0, B_HEAD_DIM)),
        _rot_tables(pos, B_ROT, (0,)),
    ])


def _rot(x, tab_ref, base, half):
    c = tab_ref[base]
    sm = tab_ref[base + 1]
    sp = tab_ref[base + 2]
    return x * c + pltpu.roll(x, LANES - half, 1) * sm + pltpu.roll(x, half, 1) * sp


SLAB_CQ, SLAB_CKV, SLAB_KPE, SLAB_BQ, SLAB_BK, SLAB_BV = 0, 2, 3, 4, 8, 9
SLAB_IQ, SLAB_IW, SLAB_IK, SLAB_GA, N_SLAB_SMALL = 10, 14, 15, 16, 16


def _cols(s0, n=1):
    return slice(s0 * LANES, (s0 + n) * LANES)


def _inproj_kernel(x_ref, ln1_ref, win_ref, wg_ref, gq_ref, gkv_ref, wuq_ref, gs_ref, tab_ref,
                   lat_ref, kpes_ref, bk_ref, bv_ref, ik_ref, qa_ref, qb_ref, iq_ref, iw_ref,
                   ga_ref, gb_ref):
    d = x_ref.shape[1]
    x = x_ref[...]
    h = x * lax.rsqrt(jnp.mean(x * x, axis=-1, keepdims=True) + EPS) * ln1_ref[...]
    hb = h.astype(BF16)
    lane = lax.broadcasted_iota(jnp.int32, (1, LANES), 1)
    lo = lane < B_HEAD_DIM

    def proj(s0, n=1):
        return jnp.dot(hb, win_ref[:, _cols(s0, n)], preferred_element_type=F32)

    def seg_rs(x2, mask, n):
        return lax.rsqrt(jnp.sum(jnp.where(mask, x2, 0.0), axis=-1, keepdims=True) * (1.0 / n) + EPS)

    cq = proj(SLAB_CQ, 2)
    cq = cq * lax.rsqrt(jnp.mean(cq * cq, axis=-1, keepdims=True) + EPS) * gq_ref[...]
    q = jnp.dot(cq.astype(BF16), wuq_ref[...], preferred_element_type=F32)
    rope_m = jnp.logical_and(lane >= A_NOPE, lane < A_NOPE + A_ROPE)
    for hd in range(A_HEADS):
        s = q[:, _cols(hd)]
        s2 = s * s
        sc = jnp.where(lo, seg_rs(s2, lo, A_NOPE), seg_rs(s2, rope_m, A_ROPE))
        qa_ref[:, _cols(hd)] = _rot(s * sc * gs_ref[0:1, :], tab_ref, 0, A_ROPE // 2).astype(BF16)

    ckv = proj(SLAB_CKV)
    lat_ref[...] = ckv * lax.rsqrt(jnp.mean(ckv * ckv, axis=-1, keepdims=True) + EPS) * gkv_ref[...]
    kp = proj(SLAB_KPE)
    kp = kp * seg_rs(kp * kp, rope_m, A_ROPE) * gs_ref[1:2, :]
    kpes_ref[...] = _rot(kp, tab_ref, 0, A_ROPE // 2)

    for j in range(B_HEADS // 2):
        s = proj(SLAB_BQ + j)
        s2 = s * s
        sc = jnp.where(lo, seg_rs(s2, lo, B_HEAD_DIM), seg_rs(s2, jnp.logical_not(lo), B_HEAD_DIM))
        s = _rot(s * sc * gs_ref[2:3, :], tab_ref, 3, B_ROT // 2) * B_SCALE
        qb_ref[:, _cols(j)] = s.astype(BF16)
    s = proj(SLAB_BK)
    s = s * seg_rs(s * s, lo, B_HEAD_DIM) * gs_ref[3:4, :]
    bk_ref[...] = _rot(s, tab_ref, 6, B_ROT // 2)[:, :B_HEAD_DIM]
    bv_ref[...] = proj(SLAB_BV)[:, :B_HEAD_DIM]

    for j in range(IDX_HEADS // 2):
        iq_ref[:, _cols(j)] = _rot(proj(SLAB_IQ + j), tab_ref, 3, IDX_ROT // 2).astype(BF16)
    iw_ref[...] = proj(SLAB_IW) * IDX_W_SCALE
    ik_ref[...] = _rot(proj(SLAB_IK), tab_ref, 6, IDX_ROT // 2)[:, :IDX_DIM]

    nd = d // LANES
    for j in range(nd):
        ga_ref[:, _cols(j)] = jax.nn.sigmoid(
            jnp.dot(hb, wg_ref[:, _cols(j)], preferred_element_type=F32))
        gb_ref[:, _cols(j)] = jax.nn.sigmoid(
            jnp.dot(hb, wg_ref[:, _cols(nd + j)], preferred_element_type=F32))


def _in_proj(x, tabs, reps, tm, w):
    n, d = x.shape
    r = tabs.shape[1]
    npb = r // tm
    assert r % tm == 0 and n == reps * r
    tok = lambda width: pl.BlockSpec((tm, width), lambda p, b: (b * npb + p, 0))
    full = lambda a: pl.BlockSpec(a.shape, lambda p, b: (0,) * a.ndim)
    outs = [
        ((n, A_KVRANK), F32), ((n, LANES), F32), ((n, B_HEAD_DIM), F32), ((n, B_HEAD_DIM), F32),
        ((n, IDX_DIM), F32), ((n, A_HEADS * LANES), BF16), ((n, B_HEADS * B_HEAD_DIM), BF16),
        ((n, IDX_HEADS * IDX_DIM), BF16), ((n, LANES), F32), ((n, d), F32), ((n, d), F32),
    ]
    return pl.pallas_call(
        _inproj_kernel,
        grid=(npb, reps),
        in_specs=[tok(d), full(w['ln1']), full(w['win']), full(w['wg']), full(w['gq']), full(w['gkv']),
                  full(w['wuq']), full(w['gs']),
                  pl.BlockSpec((tabs.shape[0], tm, LANES), lambda p, b: (0, p, 0))],
        out_specs=[tok(s[1]) for s, _ in outs],
        out_shape=[jax.ShapeDtypeStruct(s, t) for s, t in outs],
        compiler_params=_cparams("arbitrary", "arbitrary"),
        name="in_proj",
    )(x, w['ln1'], w['win'], w['wg'], w['gq'], w['gkv'], w['wuq'], w['gs'], tabs)


def _kvprep_kernel(lat_ref, kpes_ref, wk_ref, wv_ref, gk_ref, ka_ref, va_ref):
    lat = lat_ref[...].astype(BF16)
    kk = jnp.dot(lat, wk_ref[...], preferred_element_type=F32)
    vv = jnp.dot(lat, wv_ref[...], preferred_element_type=F32)
    kpes = kpes_ref[...]
    for hd in range(A_HEADS):
        s = kk[:, _cols(hd)]
        rs = lax.rsqrt(jnp.sum(s * s, axis=-1, keepdims=True) * (1.0 / A_NOPE) + EPS)
        ka_ref[hd] = (s * rs * gk_ref[...] + kpes).astype(BF16)
        va_ref[hd] = vv[:, _cols(hd)].astype(BF16)


def _kv_prep(lat_keys, kpes_keys, w):
    b, s, _ = lat_keys.shape
    ts = _pick_tile(s, 512)
    row = pl.BlockSpec((None, ts, LANES), lambda i, j: (i, j, 0))
    full = lambda a: pl.BlockSpec(a.shape, lambda i, j: (0,) * a.ndim)
    hd = pl.BlockSpec((None, A_HEADS, ts, LANES), lambda i, j: (i, 0, j, 0))
    shp = jax.ShapeDtypeStruct((b, A_HEADS, s, LANES), BF16)
    return pl.pallas_call(
        _kvprep_kernel,
        grid=(b, s // ts),
        in_specs=[row, row, full(w['wk']), full(w['wv']), full(w['gk'])],
        out_specs=[hd, hd],
        out_shape=[shp, shp],
        compiler_params=_cparams("arbitrary", "arbitrary"),
        name="kv_prep",
    )(lat_keys, kpes_keys, w['wk'], w['wv'], w['gk'])


def _mla_kernel(q_ref, k_ref, v_ref, qc_ref, kc_ref, o_ref):
    s = lax.dot_general(q_ref[...], k_ref[...], NT_DIMS, preferred_element_type=F32) * A_SCALE
    s = jnp.where(kc_ref[...] <= qc_ref[...], s, NEG_BIG)
    p = jnp.exp(s - jnp.max(s, axis=-1, keepdims=True))
    p = p * (1.0 / jnp.sum(p, axis=-1, keepdims=True))
    o_ref[...] = jnp.dot(p.astype(BF16), v_ref[...], preferred_element_type=F32).astype(BF16)


def _mla_attn(qa, ka, va, qc, kc, tq, q0, nqg, s):
    n = qa.shape[0]
    b = ka.shape[0]
    nq = n // (b * tq)
    qspec = pl.BlockSpec((tq, LANES), lambda i, h, j: (i * nq + q0 + j, h))
    kspec = pl.BlockSpec((None, None, s, LANES), lambda i, h, j: (i, h, 0, 0))
    return pl.pallas_call(
        _mla_kernel,
        grid=(b, A_HEADS, nqg),
        in_specs=[qspec, kspec, kspec,
                  pl.BlockSpec((tq, 1), lambda i, h, j: (q0 + j, 0)),
                  pl.BlockSpec((1, s), lambda i, h, j: (0, 0))],
        out_specs=pl.BlockSpec((tq, LANES), lambda i, h, j: (i * nqg + j, h)),
        out_shape=jax.ShapeDtypeStruct((b * nqg * tq, A_HEADS * LANES), BF16),
        compiler_params=_cparams("arbitrary", "arbitrary", "arbitrary"),
        name="mla_attn",
    )(qa, ka, va, qc, kc)


def _dsa_kernel(iq_ref, iw_ref, qb_ref, ik2_ref, bk2_ref, bv2_ref, qc_ref, kc_ref, kidx_ref,
                o_ref, key_s, bias_s, *, k_sel, idx_bits):
    tq = key_s.shape[0]
    adm = kc_ref[...] <= qc_ref[...]
    iw = iw_ref[...]
    score = None
    for j in range(IDX_HEADS // 2):
        iqj = iq_ref[:, _cols(j)]
        for half in range(2):
            hd = 2 * j + half
            rel = jnp.maximum(
                lax.dot_general(iqj, ik2_ref[half], NT_DIMS, preferred_element_type=F32), 0.0)
            term = iw[:, hd:hd + 1] * rel
            score = term if score is None else score + term

    bits = lax.bitcast_convert_type(score, jnp.int32)
    key = jnp.where(bits < 0, bits ^ jnp.int32(0x7FFFFFFF), bits)
    key = jnp.where(key == -1, 0, key)
    key_s[...] = jnp.where(adm, key, INT_MIN)

    kf = float(k_sel)

    def count(mask):
        return jnp.sum(jnp.where(mask, 1.0, 0.0), axis=-1, keepdims=True)

    def value_step(i, cur):
        cand = cur + lax.shift_left(jnp.int32(1), 31 - i)
        return jnp.where(count(key_s[...] >= cand) >= kf, cand, cur)

    tau = lax.fori_loop(0, 32, value_step, jnp.full((tq, 1), INT_MIN, jnp.int32))
    key = key_s[...]
    ge = key >= tau
    c_ge = count(ge)
    bias_s[...] = jnp.where(jnp.logical_and(ge, adm), 0.0, NEG_BIG)
    amb = jnp.logical_and(tau > INT_MIN, c_ge > kf)

    @pl.when(jnp.max(jnp.where(amb, 1.0, 0.0)) > 0.0)
    def _():
        key = key_s[...]
        gt = key > tau
        tie = key == tau
        need = kf - count(gt)
        kidx = kidx_ref[...]

        def index_step(i, cur):
            cand = cur + lax.shift_left(jnp.int32(1), idx_bits - 1 - i)
            below = count(jnp.logical_and(tie, kidx < cand))
            return jnp.where(below < need, cand, cur)

        last = lax.fori_loop(0, idx_bits, index_step, jnp.zeros((tq, 1), jnp.int32))
        sel = jnp.logical_or(gt, jnp.logical_and(tie, kidx <= last))
        bias_s[...] = jnp.where(jnp.logical_and(sel, adm), 0.0, NEG_BIG)

    for j in range(B_HEADS // 2):
        qbj = qb_ref[:, _cols(j)]
        acc = None
        for half in range(2):
            lg = lax.dot_general(qbj, bk2_ref[half], NT_DIMS, preferred_element_type=F32)
            lg = lg + bias_s[...]
            p = jnp.exp(lg - jnp.max(lg, axis=-1, keepdims=True))
            p = p * (1.0 / jnp.sum(p, axis=-1, keepdims=True))
            o = jnp.dot(p.astype(BF16), bv2_ref[half], preferred_element_type=F32)
            acc = o if acc is None else acc + o
        o_ref[:, _cols(j)] = acc.astype(BF16)


def _dsa_attn(iq, iw, qb, ik2, bk2, bv2, qc, kc, kidx, tq, k_sel, q0, nqg, s):
    n = iq.shape[0]
    b, _, s_all, _ = ik2.shape
    nq = n // (b * tq)
    idx_bits = max(1, int(np.ceil(np.log2(s_all + N_META + 1))))
    tok = lambda width: pl.BlockSpec((tq, width), lambda i, j: (i * nq + q0 + j, 0))
    kspec = pl.BlockSpec((None, 2, s, LANES), lambda i, j: (i, 0, 0, 0))
    row = pl.BlockSpec((1, s), lambda i, j: (0, 0))
    return pl.pallas_call(
        functools.partial(_dsa_kernel, k_sel=k_sel, idx_bits=idx_bits),
        grid=(b, nqg),
        in_specs=[tok(iq.shape[1]), tok(LANES), tok(qb.shape[1]), kspec, kspec, kspec,
                  pl.BlockSpec((tq, 1), lambda i, j: (q0 + j, 0)), row, row],
        out_specs=pl.BlockSpec((tq, qb.shape[1]), lambda i, j: (i * nqg + j, 0)),
        out_shape=jax.ShapeDtypeStruct((b * nqg * tq, qb.shape[1]), BF16),
        scratch_shapes=[pltpu.VMEM((tq, s), jnp.int32), pltpu.VMEM((tq, s), F32)],
        compiler_params=_cparams("arbitrary", "arbitrary"),
        name="dsa_attn",
    )(iq, iw, qb, ik2, bk2, bv2, qc, kc, kidx)


def _outproj_kernel(x_ref, oa_ref, ob_ref, ga_ref, gb_ref, woa_ref, wob_ref, wout_ref, ln2_ref,
                    y_ref, t_ref):
    ya = jnp.dot(oa_ref[...], woa_ref[...], preferred_element_type=F32)
    yb = jnp.dot(ob_ref[...], wob_ref[...], preferred_element_type=F32)
    mg = ga_ref[...] * ya + gb_ref[...] * yb
    y = x_ref[...] + jnp.dot(mg.astype(BF16), wout_ref[...], preferred_element_type=F32)
    y_ref[...] = y
    t = y * lax.rsqrt(jnp.mean(y * y, axis=-1, keepdims=True) + EPS) * ln2_ref[...]
    t_ref[...] = t.astype(BF16)


def _out_proj(x, oa, ob, ga, gb, w, tm):
    n, d = x.shape
    tok = lambda a: pl.BlockSpec((tm, a.shape[1]), lambda i: (i, 0))
    full = lambda a: pl.BlockSpec(a.shape, lambda i: (0,) * a.ndim)
    return pl.pallas_call(
        _outproj_kernel,
        grid=(n // tm,),
        in_specs=[tok(x), tok(oa), tok(ob), tok(ga), tok(gb),
                  full(w['woa']), full(w['wob']), full(w['wout']), full(w['ln2'])],
        out_specs=[tok(x), tok(x)],
        out_shape=[jax.ShapeDtypeStruct((n, d), F32), jax.ShapeDtypeStruct((n, d), BF16)],
        compiler_params=_cparams("arbitrary"),
        name="out_proj",
    )(x, oa, ob, ga, gb, w['woa'], w['wob'], w['wout'], w['ln2'])


def _top_rows(s, k):
    t = s.shape[1]
    row = lax.broadcasted_iota(jnp.int32, (k, t), 0).astype(F32)
    out = jnp.zeros((k, t), F32)
    seen = jnp.zeros((1, t), F32)
    for _ in range(k):
        m = jnp.max(s, axis=0, keepdims=True)
        eq = s == m
        upto = seen + jnp.sum(jnp.where(eq, 1.0, 0.0), axis=0, keepdims=True)
        out = jnp.where(jnp.logical_and(row >= seen, row < upto), m, out)
        seen = upto
        s = jnp.where(eq, -jnp.inf, s)
    return out


def _top_pair_sums(a, b):
    k = PEER_TOPK
    assert k == 16 and a.shape[0] == k
    row = lax.broadcasted_iota(jnp.int32, (8, a.shape[1]), 0)
    blocks = [a[0:1] + b]
    for i in range(1, 8):
        blocks.append(jnp.where(row < k // (i + 1), a[i:i + 1] + b[:8], -jnp.inf))
    blocks.append(a[8:] + b[0:1])
    return _top_rows(jnp.concatenate(blocks, axis=0), k)


def _peer_kernel(t_ref, y_ref, wqt_ref, sk1_ref, sk2_ref, u_ref, vt_ref, o_ref,
                 th_s, s2_s, a_s, b_s, gw_s, acc_s):
    c = pl.program_id(1)
    nc = pl.num_programs(1)
    te = u_ref.shape[0]
    half = PEER_DKEY // 2

    @pl.when(c == 0)
    def _route():
        qt = lax.dot_general(wqt_ref[...], t_ref[...], NT_DIMS, preferred_element_type=F32)
        qt = qt.astype(BF16)
        for hd in range(PEER_HEADS):
            q1 = qt[hd * PEER_DKEY:hd * PEER_DKEY + half]
            q2 = qt[hd * PEER_DKEY + half:(hd + 1) * PEER_DKEY]
            s1 = jnp.dot(sk1_ref[hd], q1, preferred_element_type=F32)
            s2 = jnp.dot(sk2_ref[hd], q2, preferred_element_type=F32)
            top1 = _top_rows(s1, PEER_TOPK)
            top2 = _top_rows(s2, PEER_TOPK)
            top = _top_pair_sums(top1, top2)
            tau = top[PEER_TOPK - 1:PEER_TOPK]
            z = jnp.sum(jnp.exp(top - top[0:1]), axis=0, keepdims=True)
            th = jnp.full(s1.shape, jnp.inf, F32)
            for j in range(PEER_TOPK):
                b = top2[j:j + 1]
                th = jnp.where(s1 + b >= tau, b, th)
            th_s[hd] = th
            s2_s[hd] = s2
            a_s[hd] = jnp.exp(s1 - top1[0:1])
            b_s[hd] = jnp.exp(s2 - top2[0:1]) * (1.0 / z)
        acc_s[...] = jnp.zeros_like(acc_s)

    tot = None
    for j in range(te // PEER_SUB):
        rows = slice(j * PEER_SUB, (j + 1) * PEER_SUB)
        at = lax.dot_general(u_ref[rows, :], t_ref[...], NT_DIMS, preferred_element_type=F32)
        for k in range(PEER_SUB // N_KEYS):
            i1 = c * (te // N_KEYS) + j * (PEER_SUB // N_KEYS) + k
            w = None
            for hd in range(PEER_HEADS):
                sel = s2_s[hd] >= th_s[hd, pl.ds(i1, 1), :]
                term = jnp.where(sel, b_s[hd] * a_s[hd, pl.ds(i1, 1), :], 0.0)
                w = term if w is None else w + term
            g = jax.nn.gelu(at[k * N_KEYS:(k + 1) * N_KEYS])
            gw_s[j * PEER_SUB + k * N_KEYS:j * PEER_SUB + (k + 1) * N_KEYS, :] = (w * g).astype(BF16)
        part = jnp.dot(vt_ref[:, rows], gw_s[rows, :], preferred_element_type=F32)
        tot = part if tot is None else tot + part
    acc_s[...] += tot

    @pl.when(c == nc - 1)
    def _finish():
        o_ref[...] = y_ref[...] + acc_s[...].T


def _peer(t, y, w, tm, te):
    n, d = y.shape
    nc = w['u'].shape[0] // te
    full = lambda a: pl.BlockSpec(a.shape, lambda i, c: (0,) * a.ndim)
    tok = pl.BlockSpec((tm, d), lambda i, c: (i, 0))
    hk = (PEER_HEADS, N_KEYS, tm)
    return pl.pallas_call(
        _peer_kernel,
        grid=(n // tm, nc),
        in_specs=[tok, tok, full(w['wqt']), full(w['sk1']), full(w['sk2']),
                  pl.BlockSpec((te, d), lambda i, c: (c, 0)),
                  pl.BlockSpec((d, te), lambda i, c: (0, c))],
        out_specs=tok,
        out_shape=jax.ShapeDtypeStruct((n, d), F32),
        scratch_shapes=[pltpu.VMEM(hk, F32), pltpu.VMEM(hk, F32), pltpu.VMEM(hk, F32),
                        pltpu.VMEM(hk, F32), pltpu.VMEM((te, tm), BF16), pltpu.VMEM((d, tm), F32)],
        compiler_params=_cparams("arbitrary", "arbitrary"),
        name="peer",
    )(t, y, w['wqt'], w['sk1'], w['sk2'], w['u'], w['vt'])


def _prep_weights(ln1_g, w_in, a_q_norm_g, a_kv_norm_g, a_w_uq, a_w_ukv, a_qk_g, b_qk_g,
                  w_o_a, w_o_b, w_out, ln2_g, peer_wq, peer_subkeys, peer_u, peer_v):
    d = w_in.shape[0]
    offs = np.cumsum(IN_SPLITS).tolist()
    cq, ckv, kpe, bq, bk, bv, iq, iw, ik, gates = jnp.split(w_in, offs, axis=1)
    z = lambda n: jnp.zeros((d, n), w_in.dtype)
    win = jnp.concatenate([
        cq, ckv, z(A_NOPE), kpe, z(LANES - A_NOPE - A_ROPE), bq,
        bk, z(LANES - B_HEAD_DIM), bv, z(LANES - B_HEAD_DIM), iq,
        iw, z(LANES - IDX_HEADS), ik, z(LANES - IDX_DIM)], axis=1).astype(BF16)
    assert win.shape[1] == N_SLAB_SMALL * LANES

    def pad_last(a, n):
        return jnp.pad(a, [(0, 0)] * (a.ndim - 1) + [(0, n - a.shape[-1])])

    wuq = pad_last(a_w_uq.reshape(A_QRANK, A_HEADS, A_NOPE + A_ROPE), LANES)
    wuq = wuq.reshape(A_QRANK, A_HEADS * LANES).astype(BF16)
    ukv = a_w_ukv.reshape(A_KVRANK, A_HEADS, A_NOPE + A_V)
    wk = pad_last(ukv[..., :A_NOPE], LANES).reshape(A_KVRANK, A_HEADS * LANES).astype(BF16)
    wv = pad_last(ukv[..., A_NOPE:], LANES).reshape(A_KVRANK, A_HEADS * LANES).astype(BF16)
    woa = pad_last(w_o_a.reshape(A_HEADS, A_V, d).transpose(0, 2, 1), LANES)
    woa = woa.transpose(0, 2, 1).reshape(A_HEADS * LANES, d).astype(BF16)

    row = lambda v: pad_last(v, LANES)[None, :]
    gs = jnp.concatenate([
        row(a_qk_g[0]),
        row(jnp.concatenate([jnp.zeros((A_NOPE,), F32), a_qk_g[1, A_NOPE:]])),
        row(jnp.concatenate([b_qk_g[0], b_qk_g[0]])),
        row(b_qk_g[1]),
        jnp.zeros((4, LANES), F32)], axis=0)
    return dict(
        ln1=ln1_g[None, :], win=win, wg=gates.astype(BF16), gq=a_q_norm_g[None, :],
        gkv=a_kv_norm_g[None, :], wuq=wuq, gs=gs, wk=wk, wv=wv, gk=row(a_qk_g[1, :A_NOPE]),
        woa=woa, wob=w_o_b.astype(BF16), wout=w_out.astype(BF16), ln2=ln2_g[None, :],
        wqt=peer_wq.T.astype(BF16),
        sk1=peer_subkeys[:, 0].astype(BF16), sk2=peer_subkeys[:, 1].astype(BF16),
        u=peer_u.astype(BF16), vt=peer_v.T.astype(BF16))


def _key_layout(meta, rows):
    pad = jnp.zeros((meta.shape[0], LANES - meta.shape[1], meta.shape[2]), meta.dtype)
    return jnp.concatenate([meta, pad, rows], axis=1)


def _pair_slabs(k):
    kb = k.astype(BF16)
    z = jnp.zeros_like(kb)
    return jnp.stack([jnp.concatenate([kb, z], -1), jnp.concatenate([z, kb], -1)], axis=1)


def _mixers_and_peer(x, rows, keys, qc, kc, kidx, k_sel, w, tq, tm, te, groups):
    lat_k, kpes_k, bk_k, bv_k, ik_k = keys
    (_, _, _, _, _, qa, qb, iq, iw, ga, gb) = rows
    b = lat_k.shape[0]
    ka, va = _kv_prep(lat_k, kpes_k, w)
    ik2, bk2, bv2 = _pair_slabs(ik_k), _pair_slabs(bk_k), _pair_slabs(bv_k)
    oa, ob = [], []
    for q0, nqg, s in groups:
        o = _mla_attn(qa, ka, va, qc, kc, tq, q0, nqg, s)
        oa.append(o.reshape(b, nqg * tq, o.shape[-1]))
        o = _dsa_attn(iq, iw, qb, ik2, bk2, bv2, qc, kc, kidx, tq, k_sel, q0, nqg, s)
        ob.append(o.reshape(b, nqg * tq, o.shape[-1]))
    oa = jnp.concatenate(oa, axis=1).reshape(x.shape[0], -1)
    ob = jnp.concatenate(ob, axis=1).reshape(x.shape[0], -1)
    y1, t2 = _out_proj(x, oa, ob, ga, gb, w, tm)
    tp = 2 * tm if x.shape[0] % (2 * tm) == 0 else tm
    return _peer(t2, y1, w, tp, te)


def kernel(x_prompt, x_sample, cache_a_latent, cache_a_kpe, cache_b_k, cache_b_v, cache_b_idx_k,
           meta_tokens, ln1_g, w_in, a_q_norm_g, a_kv_norm_g, a_w_uq, a_w_ukv, a_qk_g, b_qk_g,
           w_o_a, w_o_b, w_out, ln2_g, peer_wq, peer_subkeys, peer_u, peer_v):
    bp, sp, d = x_prompt.shape
    bs, ts, _ = x_sample.shape
    depth, _, past, _ = cache_a_latent.shape
    assert depth == 1 and sp % CHUNK == 0
    w = _prep_weights(ln1_g[0], w_in[0], a_q_norm_g[0], a_kv_norm_g[0], a_w_uq[0], a_w_ukv[0],
                      a_qk_g[0], b_qk_g[0], w_o_a[0], w_o_b[0], w_out[0], ln2_g[0], peer_wq[0],
                      peer_subkeys[0], peer_u[0], peer_v[0])
    tm = 256
    te = 512
    i32 = jnp.int32

    xq = x_prompt.reshape(bp * sp, d)
    rows_p = _in_proj(xq, _all_tables(N_META + jnp.arange(sp, dtype=i32)), bp, tm, w)
    rows_m = _in_proj(meta_tokens.astype(F32), _all_tables(jnp.arange(N_META, dtype=i32)), 1, N_META, w)
    xs = x_sample.reshape(bs * ts, d)
    pos_s = past + jnp.arange(ts, dtype=i32)
    rows_s = _in_proj(xs, _all_tables(jnp.tile(pos_s, bs)), 1, bs * ts, w)

    s_p = LANES + sp
    per_b = lambda a, b, t: a.reshape(b, t, a.shape[-1])
    meta_b = lambda a: jnp.broadcast_to(a[None], (bp,) + a.shape)
    keys_p = tuple(_key_layout(meta_b(m), per_b(r, bp, sp)) for r, m in zip(rows_p[:5], rows_m[:5]))
    chunk_q = jnp.arange(sp, dtype=i32) // CHUNK
    n_pad = LANES - N_META
    kc_p = jnp.concatenate([jnp.full((N_META,), -1, i32), jnp.full((n_pad,), PAD_CHUNK, i32),
                            chunk_q])[None, :]
    kidx_p = jnp.concatenate([jnp.arange(N_META, dtype=i32), s_p + jnp.arange(n_pad, dtype=i32),
                              N_META + jnp.arange(sp, dtype=i32)])[None, :]
    tq = 256
    nq = sp // tq
    tpg = max(1, nq // PROMPT_GROUPS)
    groups_p = [(q0, min(tpg, nq - q0), LANES + min(q0 + tpg, nq) * tq) for q0 in range(0, nq, tpg)]
    y_p = _mixers_and_peer(xq, rows_p, keys_p, chunk_q[:, None], kc_p, kidx_p,
                           min(IDX_TOPK, sp // 4), w, tq, tm, te, groups_p)

    s_s = _round_up(past + ts, LANES)
    kpes_cache = jnp.pad(cache_a_kpe[0], ((0, 0), (0, 0), (A_NOPE, LANES - A_NOPE - A_ROPE)))
    caches = (cache_a_latent[0], kpes_cache, cache_b_k[0], cache_b_v[0], cache_b_idx_k[0])
    keys_s = tuple(jnp.pad(jnp.concatenate([c.astype(F32), per_b(r, bs, ts)], axis=1),
                           ((0, 0), (0, s_s - past - ts), (0, 0)))
                   for c, r in zip(caches, rows_s[:5]))
    kc_s = jnp.concatenate([jnp.zeros((past,), i32), jnp.ones((ts,), i32),
                            jnp.full((s_s - past - ts,), PAD_CHUNK, i32)])[None, :]
    kidx_s = jnp.arange(s_s, dtype=i32)[None, :]
    y_s = _mixers_and_peer(xs, rows_s, keys_s, jnp.ones((ts, 1), i32), kc_s, kidx_s,
                           min(IDX_TOPK, (past + ts) // 4), w, ts, bs * ts, te, [(0, 1, s_s)])

    kpe_of = lambda r: r[:, A_NOPE:A_NOPE + A_ROPE]

    def new_rows_p(r, m):
        return jnp.concatenate([meta_b(m), per_b(r, bp, sp)], axis=1)[None]

    outs_p = [new_rows_p(rows_p[0], rows_m[0]), new_rows_p(kpe_of(rows_p[1]), kpe_of(rows_m[1]))]
    outs_p += [new_rows_p(rows_p[i], rows_m[i]) for i in (2, 3, 4)]
    outs_s = [per_b(rows_s[0], bs, ts)[None], per_b(kpe_of(rows_s[1]), bs, ts)[None]]
    outs_s += [per_b(rows_s[i], bs, ts)[None] for i in (2, 3, 4)]
    return (y_p.reshape(bp, sp, d), y_s.reshape(bs, ts, d), *outs_p, *outs_s)
```

```python
import functools

import jax
import jax.numpy as jnp
import numpy as np
from jax import lax
from jax.experimental import pallas as pl
from jax.experimental.pallas import tpu as pltpu

F32 = jnp.float32
BF16 = jnp.bfloat16

LANES = 128
VMEM_LIMIT = 52 * 1024 * 1024

CHUNK = 64
N_META = 16
ROPE_THETA = 500000.0
EPS = 1e-6

A_HEADS = 8
A_NOPE = 64
A_ROPE = 32
A_V = 64
A_QRANK = 256
A_KVRANK = 128
A_SCALE = (A_NOPE + A_ROPE) ** -0.5

B_HEADS = 8
B_HEAD_DIM = 64
B_ROT = B_HEAD_DIM // 4
B_SCALE = B_HEAD_DIM ** -0.5
IDX_HEADS = 8
IDX_DIM = 64
IDX_ROT = IDX_DIM // 4
IDX_TOPK = 256
IDX_W_SCALE = (IDX_HEADS * IDX_DIM) ** -0.5

PEER_HEADS = 8
PEER_DKEY = 128
N_KEYS = 128
PEER_TOPK = 16
PROMPT_GROUPS = 8

IN_SPLITS = (A_QRANK, A_KVRANK, A_ROPE, B_HEADS * B_HEAD_DIM, B_HEAD_DIM, B_HEAD_DIM,
             IDX_HEADS * IDX_DIM, IDX_HEADS, IDX_DIM)

NEG_BIG = -1e30
INT_MIN = -2147483648
PAD_CHUNK = 1 << 30

NT_DIMS = (((1,), (1,)), ((), ()))


def _cparams(*sem):
    return pltpu.CompilerParams(dimension_semantics=sem, vmem_limit_bytes=VMEM_LIMIT)


def _round_up(n, m):
    return -(-n // m) * m


def _pick_tile(n, target):
    best = LANES
    for t in range(LANES, min(n, target) + 1, LANES):
        if n % t == 0:
            best = t
    return best


def _rot_tables(pos, rot_dim, offsets):
    half = rot_dim // 2
    inv = ROPE_THETA ** (-jnp.arange(half, dtype=F32) * (2.0 / rot_dim))
    ang = pos.astype(F32)[:, None] * inv[None, :]
    cos, sin = jnp.cos(ang), jnp.sin(ang)
    t = pos.shape[0]
    c = jnp.ones((t, LANES), F32)
    sm = jnp.zeros((t, LANES), F32)
    sp = jnp.zeros((t, LANES), F32)
    for o in offsets:
        c = c.at[:, o:o + half].set(cos).at[:, o + half:o + rot_dim].set(cos)
        sm = sm.at[:, o:o + half].set(-sin)
        sp = sp.at[:, o + half:o + rot_dim].set(sin)
    return jnp.stack([c, sm, sp])


def _all_tables(pos):
    return jnp.concatenate([
        _rot_tables(pos, A_ROPE, (A_NOPE,)),
        _rot_tables(pos, B_ROT, (0, B_HEAD_DIM)),
        _rot_tables(pos, B_ROT, (0,)),
    ])


def _rot(x, tab_ref, base, half):
    c = tab_ref[base]
    sm = tab_ref[base + 1]
    sp = tab_ref[base + 2]
    return x * c + pltpu.roll(x, LANES - half, 1) * sm + pltpu.roll(x, half, 1) * sp


SLAB_CQ, SLAB_CKV, SLAB_KPE, SLAB_BQ, SLAB_BK, SLAB_BV = 0, 2, 3, 4, 8, 9
SLAB_IQ, SLAB_IW, SLAB_IK, SLAB_GA, N_SLAB_SMALL = 10, 14, 15, 16, 16


def _cols(s0, n=1):
    return slice(s0 * LANES, (s0 + n) * LANES)


def _inproj_kernel(x_ref, ln1_ref, win_ref, wg_ref, gq_ref, gkv_ref, wuq_ref, gs_ref, tab_ref,
                   lat_ref, kpes_ref, bk_ref, bv_ref, ik_ref, qa_ref, qb_ref, iq_ref, iw_ref,
                   ga_ref, gb_ref):
    d = x_ref.shape[1]
    x = x_ref[...]
    h = x * lax.rsqrt(jnp.mean(x * x, axis=-1, keepdims=True) + EPS) * ln1_ref[...]
    hb = h.astype(BF16)
    lane = lax.broadcasted_iota(jnp.int32, (1, LANES), 1)
    lo = lane < B_HEAD_DIM

    def proj(s0, n=1):
        return jnp.dot(hb, win_ref[:, _cols(s0, n)], preferred_element_type=F32)

    def seg_rs(x2, mask, n):
        return lax.rsqrt(jnp.sum(jnp.where(mask, x2, 0.0), axis=-1, keepdims=True) * (1.0 / n) + EPS)

    cq = proj(SLAB_CQ, 2)
    cq = cq * lax.rsqrt(jnp.mean(cq * cq, axis=-1, keepdims=True) + EPS) * gq_ref[...]
    q = jnp.dot(cq.astype(BF16), wuq_ref[...], preferred_element_type=F32)
    rope_m = jnp.logical_and(lane >= A_NOPE, lane < A_NOPE + A_ROPE)
    for hd in range(A_HEADS):
        s = q[:, _cols(hd)]
        s2 = s * s
        sc = jnp.where(lo, seg_rs(s2, lo, A_NOPE), seg_rs(s2, rope_m, A_ROPE))
        qa_ref[:, _cols(hd)] = _rot(s * sc * gs_ref[0:1, :], tab_ref, 0, A_ROPE // 2).astype(BF16)

    ckv = proj(SLAB_CKV)
    lat_ref[...] = ckv * lax.rsqrt(jnp.mean(ckv * ckv, axis=-1, keepdims=True) + EPS) * gkv_ref[...]
    kp = proj(SLAB_KPE)
    kp = kp * seg_rs(kp * kp, rope_m, A_ROPE) * gs_ref[1:2, :]
    kpes_ref[...] = _rot(kp, tab_ref, 0, A_ROPE // 2)

    for j in range(B_HEADS // 2):
        s = proj(SLAB_BQ + j)
        s2 = s * s
        sc = jnp.where(lo, seg_rs(s2, lo, B_HEAD_DIM), seg_rs(s2, jnp.logical_not(lo), B_HEAD_DIM))
        s = _rot(s * sc * gs_ref[2:3, :], tab_ref, 3, B_ROT // 2) * B_SCALE
        qb_ref[:, _cols(j)] = s.astype(BF16)
    s = proj(SLAB_BK)
    s = s * seg_rs(s * s, lo, B_HEAD_DIM) * gs_ref[3:4, :]
    bk_ref[...] = _rot(s, tab_ref, 6, B_ROT // 2)[:, :B_HEAD_DIM]
    bv_ref[...] = proj(SLAB_BV)[:, :B_HEAD_DIM]

    for j in range(IDX_HEADS // 2):
        iq_ref[:, _cols(j)] = _rot(proj(SLAB_IQ + j), tab_ref, 3, IDX_ROT // 2).astype(BF16)
    iw_ref[...] = proj(SLAB_IW) * IDX_W_SCALE
    ik_ref[...] = _rot(proj(SLAB_IK), tab_ref, 6, IDX_ROT // 2)[:, :IDX_DIM]

    nd = d // LANES
    for j in range(nd):
        ga_ref[:, _cols(j)] = jax.nn.sigmoid(
            jnp.dot(hb, wg_ref[:, _cols(j)], preferred_element_type=F32))
        gb_ref[:, _cols(j)] = jax.nn.sigmoid(
            jnp.dot(hb, wg_ref[:, _cols(nd + j)], preferred_element_type=F32))


def _in_proj(x, tabs, reps, tm, w):
    n, d = x.shape
    r = tabs.shape[1]
    npb = r // tm
    assert r % tm == 0 and n == reps * r
    tok = lambda width: pl.BlockSpec((tm, width), lambda p, b: (b * npb + p, 0))
    full = lambda a: pl.BlockSpec(a.shape, lambda p, b: (0,) * a.ndim)
    outs = [
        ((n, A_KVRANK), F32), ((n, LANES), F32), ((n, B_HEAD_DIM), F32), ((n, B_HEAD_DIM), F32),
        ((n, IDX_DIM), F32), ((n, A_HEADS * LANES), BF16), ((n, B_HEADS * B_HEAD_DIM), BF16),
        ((n, IDX_HEADS * IDX_DIM), BF16), ((n, LANES), F32), ((n, d), F32), ((n, d), F32),
    ]
    return pl.pallas_call(
        _inproj_kernel,
        grid=(npb, reps),
        in_specs=[tok(d), full(w['ln1']), full(w['win']), full(w['wg']), full(w['gq']), full(w['gkv']),
                  full(w['wuq']), full(w['gs']),
                  pl.BlockSpec((tabs.shape[0], tm, LANES), lambda p, b: (0, p, 0))],
        out_specs=[tok(s[1]) for s, _ in outs],
        out_shape=[jax.ShapeDtypeStruct(s, t) for s, t in outs],
        compiler_params=_cparams("arbitrary", "arbitrary"),
        name="in_proj",
    )(x, w['ln1'], w['win'], w['wg'], w['gq'], w['gkv'], w['wuq'], w['gs'], tabs)


def _kvprep_kernel(lat_ref, kpes_ref, wk_ref, wv_ref, gk_ref, ka_ref, va_ref):
    lat = lat_ref[...].astype(BF16)
    kk = jnp.dot(lat, wk_ref[...], preferred_element_type=F32)
    vv = jnp.dot(lat, wv_ref[...], preferred_element_type=F32)
    kpes = kpes_ref[...]
    for hd in range(A_HEADS):
        s = kk[:, _cols(hd)]
        rs = lax.rsqrt(jnp.sum(s * s, axis=-1, keepdims=True) * (1.0 / A_NOPE) + EPS)
        ka_ref[hd] = (s * rs * gk_ref[...] + kpes).astype(BF16)
        va_ref[hd] = vv[:, _cols(hd)].astype(BF16)


def _kv_prep(lat_keys, kpes_keys, w):
    b, s, _ = lat_keys.shape
    ts = _pick_tile(s, 512)
    row = pl.BlockSpec((None, ts, LANES), lambda i, j: (i, j, 0))
    full = lambda a: pl.BlockSpec(a.shape, lambda i, j: (0,) * a.ndim)
    hd = pl.BlockSpec((None, A_HEADS, ts, LANES), lambda i, j: (i, 0, j, 0))
    shp = jax.ShapeDtypeStruct((b, A_HEADS, s, LANES), BF16)
    return pl.pallas_call(
        _kvprep_kernel,
        grid=(b, s // ts),
        in_specs=[row, row, full(w['wk']), full(w['wv']), full(w['gk'])],
        out_specs=[hd, hd],
        out_shape=[shp, shp],
        compiler_params=_cparams("arbitrary", "arbitrary"),
        name="kv_prep",
    )(lat_keys, kpes_keys, w['wk'], w['wv'], w['gk'])


def _mla_kernel(q_ref, k_ref, v_ref, qc_ref, kc_ref, o_ref):
    s = lax.dot_general(q_ref[...], k_ref[...], NT_DIMS, preferred_element_type=F32) * A_SCALE
    s = jnp.where(kc_ref[...] <= qc_ref[...], s, NEG_BIG)
    p = jnp.exp(s - jnp.max(s, axis=-1, keepdims=True))
    p = p * (1.0 / jnp.sum(p, axis=-1, keepdims=True))
    o_ref[...] = jnp.dot(p.astype(BF16), v_ref[...], preferred_element_type=F32).astype(BF16)


def _mla_attn(qa, ka, va, qc, kc, tq, q0, nqg, s):
    n = qa.shape[0]
    b = ka.shape[0]
    nq = n // (b * tq)
    qspec = pl.BlockSpec((tq, LANES), lambda i, h, j: (i * nq + q0 + j, h))
    kspec = pl.BlockSpec((None, None, s, LANES), lambda i, h, j: (i, h, 0, 0))
    return pl.pallas_call(
        _mla_kernel,
        grid=(b, A_HEADS, nqg),
        in_specs=[qspec, kspec, kspec,
                  pl.BlockSpec((tq, 1), lambda i, h, j: (q0 + j, 0)),
                  pl.BlockSpec((1, s), lambda i, h, j: (0, 0))],
        out_specs=pl.BlockSpec((tq, LANES), lambda i, h, j: (i * nqg + j, h)),
        out_shape=jax.ShapeDtypeStruct((b * nqg * tq, A_HEADS * LANES), BF16),
        compiler_params=_cparams("arbitrary", "arbitrary", "arbitrary"),
        name="mla_attn",
    )(qa, ka, va, qc, kc)


def _dsa_kernel(iq_ref, iw_ref, qb_ref, ik2_ref, bk2_ref, bv2_ref, qc_ref, kc_ref, kidx_ref,
                o_ref, key_s, bias_s, *, k_sel, idx_bits):
    tq = key_s.shape[0]
    adm = kc_ref[...] <= qc_ref[...]
    iw = iw_ref[...]
    score = None
    for j in range(IDX_HEADS // 2):
        iqj = iq_ref[:, _cols(j)]
        for half in range(2):
            hd = 2 * j + half
            rel = jnp.maximum(
                lax.dot_general(iqj, ik2_ref[half], NT_DIMS, preferred_element_type=F32), 0.0)
            term = iw[:, hd:hd + 1] * rel
            score = term if score is None else score + term

    bits = lax.bitcast_convert_type(score, jnp.int32)
    key = jnp.where(bits < 0, bits ^ jnp.int32(0x7FFFFFFF), bits)
    key = jnp.where(key == -1, 0, key)
    key_s[...] = jnp.where(adm, key, INT_MIN)

    kf = float(k_sel)

    def count(mask):
        return jnp.sum(jnp.where(mask, 1.0, 0.0), axis=-1, keepdims=True)

    def value_step(i, cur):
        cand = cur + lax.shift_left(jnp.int32(1), 31 - i)
        return jnp.where(count(key_s[...] >= cand) >= kf, cand, cur)

    tau = lax.fori_loop(0, 32, value_step, jnp.full((tq, 1), INT_MIN, jnp.int32))
    key = key_s[...]
    ge = key >= tau
    c_ge = count(ge)
    bias_s[...] = jnp.where(jnp.logical_and(ge, adm), 0.0, NEG_BIG)
    amb = jnp.logical_and(tau > INT_MIN, c_ge > kf)

    @pl.when(jnp.max(jnp.where(amb, 1.0, 0.0)) > 0.0)
    def _():
        key = key_s[...]
        gt = key > tau
        tie = key == tau
        need = kf - count(gt)
        kidx = kidx_ref[...]

        def index_step(i, cur):
            cand = cur + lax.shift_left(jnp.int32(1), idx_bits - 1 - i)
            below = count(jnp.logical_and(tie, kidx < cand))
            return jnp.where(below < need, cand, cur)

        last = lax.fori_loop(0, idx_bits, index_step, jnp.zeros((tq, 1), jnp.int32))
        sel = jnp.logical_or(gt, jnp.logical_and(tie, kidx <= last))
        bias_s[...] = jnp.where(jnp.logical_and(sel, adm), 0.0, NEG_BIG)

    for j in range(B_HEADS // 2):
        qbj = qb_ref[:, _cols(j)]
        acc = None
        for half in range(2):
            lg = lax.dot_general(qbj, bk2_ref[half], NT_DIMS, preferred_element_type=F32)
            lg = lg + bias_s[...]
            p = jnp.exp(lg - jnp.max(lg, axis=-1, keepdims=True))
            p = p * (1.0 / jnp.sum(p, axis=-1, keepdims=True))
            o = jnp.dot(p.astype(BF16), bv2_ref[half], preferred_element_type=F32)
            acc = o if acc is None else acc + o
        o_ref[:, _cols(j)] = acc.astype(BF16)


def _dsa_attn(iq, iw, qb, ik2, bk2, bv2, qc, kc, kidx, tq, k_sel, q0, nqg, s):
    n = iq.shape[0]
    b, _, s_all, _ = ik2.shape
    nq = n // (b * tq)
    idx_bits = max(1, int(np.ceil(np.log2(s_all + N_META + 1))))
    tok = lambda width: pl.BlockSpec((tq, width), lambda i, j: (i * nq + q0 + j, 0))
    kspec = pl.BlockSpec((None, 2, s, LANES), lambda i, j: (i, 0, 0, 0))
    row = pl.BlockSpec((1, s), lambda i, j: (0, 0))
    return pl.pallas_call(
        functools.partial(_dsa_kernel, k_sel=k_sel, idx_bits=idx_bits),
        grid=(b, nqg),
        in_specs=[tok(iq.shape[1]), tok(LANES), tok(qb.shape[1]), kspec, kspec, kspec,
                  pl.BlockSpec((tq, 1), lambda i, j: (q0 + j, 0)), row, row],
        out_specs=pl.BlockSpec((tq, qb.shape[1]), lambda i, j: (i * nqg + j, 0)),
        out_shape=jax.ShapeDtypeStruct((b * nqg * tq, qb.shape[1]), BF16),
        scratch_shapes=[pltpu.VMEM((tq, s), jnp.int32), pltpu.VMEM((tq, s), F32)],
        compiler_params=_cparams("arbitrary", "arbitrary"),
        name="dsa_attn",
    )(iq, iw, qb, ik2, bk2, bv2, qc, kc, kidx)


def _outproj_kernel(x_ref, oa_ref, ob_ref, ga_ref, gb_ref, woa_ref, wob_ref, wout_ref, ln2_ref,
                    y_ref, t_ref):
    ya = jnp.dot(oa_ref[...], woa_ref[...], preferred_element_type=F32)
    yb = jnp.dot(ob_ref[...], wob_ref[...], preferred_element_type=F32)
    mg = ga_ref[...] * ya + gb_ref[...] * yb
    y = x_ref[...] + jnp.dot(mg.astype(BF16), wout_ref[...], preferred_element_type=F32)
    y_ref[...] = y
    t = y * lax.rsqrt(jnp.mean(y * y, axis=-1, keepdims=True) + EPS) * ln2_ref[...]
    t_ref[...] = t.astype(BF16)


def _out_proj(x, oa, ob, ga, gb, w, tm):
    n, d = x.shape
    tok = lambda a: pl.BlockSpec((tm, a.shape[1]), lambda i: (i, 0))
    full = lambda a: pl.BlockSpec(a.shape, lambda i: (0,) * a.ndim)
    return pl.pallas_call(
        _outproj_kernel,
        grid=(n // tm,),
        in_specs=[tok(x), tok(oa), tok(ob), tok(ga), tok(gb),
                  full(w['woa']), full(w['wob']), full(w['wout']), full(w['ln2'])],
        out_specs=[tok(x), tok(x)],
        out_shape=[jax.ShapeDtypeStruct((n, d), F32), jax.ShapeDtypeStruct((n, d), BF16)],
        compiler_params=_cparams("arbitrary"),
        name="out_proj",
    )(x, oa, ob, ga, gb, w['woa'], w['wob'], w['wout'], w['ln2'])


def _top_rows(s, k, with_rank=False):
    t = s.shape[1]
    row = lax.broadcasted_iota(jnp.int32, (k, t), 0).astype(F32)
    out = jnp.zeros((k, t), F32)
    seen = jnp.zeros((1, t), F32)
    rank = jnp.full(s.shape, float(k), F32)
    for _ in range(k):
        m = jnp.max(s, axis=0, keepdims=True)
        eq = s == m
        if with_rank:
            rank = jnp.where(eq, seen, rank)
        upto = seen + jnp.sum(jnp.where(eq, 1.0, 0.0), axis=0, keepdims=True)
        out = jnp.where(jnp.logical_and(row >= seen, row < upto), m, out)
        seen = upto
        s = jnp.where(eq, -jnp.inf, s)
    return (out, rank) if with_rank else out


def _top_pair_sums(a, b):
    k = PEER_TOPK
    assert k == 16 and a.shape[0] == k
    row = lax.broadcasted_iota(jnp.int32, (8, a.shape[1]), 0)
    blocks = [a[0:1] + b]
    for i in range(1, 8):
        blocks.append(jnp.where(row < k // (i + 1), a[i:i + 1] + b[:8], -jnp.inf))
    blocks.append(a[8:] + b[0:1])
    return _top_rows(jnp.concatenate(blocks, axis=0), k)


def _peer_kernel(t_ref, y_ref, wqt_ref, sk1_ref, sk2_ref, u_ref, vt_ref, o_ref,
                 m_s, r_s, a_s, b_s, gw_s, acc_s):
    c = pl.program_id(1)
    nc = pl.num_programs(1)
    te = u_ref.shape[0]
    half = PEER_DKEY // 2

    @pl.when(c == 0)
    def _route():
        qt = lax.dot_general(wqt_ref[...], t_ref[...], NT_DIMS, preferred_element_type=F32)
        qt = qt.astype(BF16)
        for hd in range(PEER_HEADS):
            q1 = qt[hd * PEER_DKEY:hd * PEER_DKEY + half]
            q2 = qt[hd * PEER_DKEY + half:(hd + 1) * PEER_DKEY]
            s1 = jnp.dot(sk1_ref[hd], q1, preferred_element_type=F32)
            s2 = jnp.dot(sk2_ref[hd], q2, preferred_element_type=F32)
            top1 = _top_rows(s1, PEER_TOPK)
            top2, r = _top_rows(s2, PEER_TOPK, with_rank=True)
            top = _top_pair_sums(top1, top2)
            tau = top[PEER_TOPK - 1:PEER_TOPK]
            z = jnp.sum(jnp.exp(top - top[0:1]), axis=0, keepdims=True)
            m = jnp.zeros(s1.shape, F32)
            for j in range(PEER_TOPK):
                reach = top1 + top2[j:j + 1] >= tau
                c_j = jnp.min(jnp.where(reach, top1, jnp.inf), axis=0, keepdims=True)
                m = m + jnp.where(s1 >= c_j, 1.0, 0.0)
            m_s[hd] = m
            r_s[hd] = r.astype(BF16)
            a_s[hd] = jnp.exp(s1 - top1[0:1])
            b_s[hd] = (jnp.exp(s2 - top2[0:1]) * (1.0 / z)).astype(BF16)
        acc_s[...] = jnp.zeros_like(acc_s)

    at = lax.dot_general(u_ref[...], t_ref[...], NT_DIMS, preferred_element_type=F32)
    zero = jnp.zeros((), BF16)
    for k in range(te // N_KEYS):
        i1 = c * (te // N_KEYS) + k
        w = None
        for hd in range(PEER_HEADS):
            sel = r_s[hd] < m_s[hd, pl.ds(i1, 1), :].astype(BF16)
            term = jnp.where(sel, b_s[hd] * a_s[hd, pl.ds(i1, 1), :].astype(BF16), zero)
            w = term if w is None else w + term
        g = jax.nn.gelu(at[k * N_KEYS:(k + 1) * N_KEYS])
        gw_s[k * N_KEYS:(k + 1) * N_KEYS, :] = w * g.astype(BF16)
    acc_s[...] += jnp.dot(vt_ref[...], gw_s[...], preferred_element_type=F32)

    @pl.when(c == nc - 1)
    def _finish():
        o_ref[...] = y_ref[...] + acc_s[...].T


def _peer(t, y, w, tm, te):
    n, d = y.shape
    nc = w['u'].shape[0] // te
    full = lambda a: pl.BlockSpec(a.shape, lambda i, c: (0,) * a.ndim)
    tok = pl.BlockSpec((tm, d), lambda i, c: (i, 0))
    hk = (PEER_HEADS, N_KEYS, tm)
    return pl.pallas_call(
        _peer_kernel,
        grid=(n // tm, nc),
        in_specs=[tok, tok, full(w['wqt']), full(w['sk1']), full(w['sk2']),
                  pl.BlockSpec((te, d), lambda i, c: (c, 0)),
                  pl.BlockSpec((d, te), lambda i, c: (0, c))],
        out_specs=tok,
        out_shape=jax.ShapeDtypeStruct((n, d), F32),
        scratch_shapes=[pltpu.VMEM(hk, F32), pltpu.VMEM(hk, BF16), pltpu.VMEM(hk, F32),
                        pltpu.VMEM(hk, BF16), pltpu.VMEM((te, tm), BF16), pltpu.VMEM((d, tm), F32)],
        compiler_params=_cparams("arbitrary", "arbitrary"),
        name="peer",
    )(t, y, w['wqt'], w['sk1'], w['sk2'], w['u'], w['vt'])


def _prep_weights(ln1_g, w_in, a_q_norm_g, a_kv_norm_g, a_w_uq, a_w_ukv, a_qk_g, b_qk_g,
                  w_o_a, w_o_b, w_out, ln2_g, peer_wq, peer_subkeys, peer_u, peer_v):
    d = w_in.shape[0]
    offs = np.cumsum(IN_SPLITS).tolist()
    cq, ckv, kpe, bq, bk, bv, iq, iw, ik, gates = jnp.split(w_in, offs, axis=1)
    z = lambda n: jnp.zeros((d, n), w_in.dtype)
    win = jnp.concatenate([
        cq, ckv, z(A_NOPE), kpe, z(LANES - A_NOPE - A_ROPE), bq,
        bk, z(LANES - B_HEAD_DIM), bv, z(LANES - B_HEAD_DIM), iq,
        iw, z(LANES - IDX_HEADS), ik, z(LANES - IDX_DIM)], axis=1).astype(BF16)
    assert win.shape[1] == N_SLAB_SMALL * LANES

    def pad_last(a, n):
        return jnp.pad(a, [(0, 0)] * (a.ndim - 1) + [(0, n - a.shape[-1])])

    wuq = pad_last(a_w_uq.reshape(A_QRANK, A_HEADS, A_NOPE + A_ROPE), LANES)
    wuq = wuq.reshape(A_QRANK, A_HEADS * LANES).astype(BF16)
    ukv = a_w_ukv.reshape(A_KVRANK, A_HEADS, A_NOPE + A_V)
    wk = pad_last(ukv[..., :A_NOPE], LANES).reshape(A_KVRANK, A_HEADS * LANES).astype(BF16)
    wv = pad_last(ukv[..., A_NOPE:], LANES).reshape(A_KVRANK, A_HEADS * LANES).astype(BF16)
    woa = pad_last(w_o_a.reshape(A_HEADS, A_V, d).transpose(0, 2, 1), LANES)
    woa = woa.transpose(0, 2, 1).reshape(A_HEADS * LANES, d).astype(BF16)

    row = lambda v: pad_last(v, LANES)[None, :]
    gs = jnp.concatenate([
        row(a_qk_g[0]),
        row(jnp.concatenate([jnp.zeros((A_NOPE,), F32), a_qk_g[1, A_NOPE:]])),
        row(jnp.concatenate([b_qk_g[0], b_qk_g[0]])),
        row(b_qk_g[1]),
        jnp.zeros((4, LANES), F32)], axis=0)
    return dict(
        ln1=ln1_g[None, :], win=win, wg=gates.astype(BF16), gq=a_q_norm_g[None, :],
        gkv=a_kv_norm_g[None, :], wuq=wuq, gs=gs, wk=wk, wv=wv, gk=row(a_qk_g[1, :A_NOPE]),
        woa=woa, wob=w_o_b.astype(BF16), wout=w_out.astype(BF16), ln2=ln2_g[None, :],
        wqt=peer_wq.T.astype(BF16),
        sk1=peer_subkeys[:, 0].astype(BF16), sk2=peer_subkeys[:, 1].astype(BF16),
        u=peer_u.astype(BF16), vt=peer_v.T.astype(BF16))


def _key_layout(meta, rows):
    pad = jnp.zeros((meta.shape[0], LANES - meta.shape[1], meta.shape[2]), meta.dtype)
    return jnp.concatenate([meta, pad, rows], axis=1)


def _pair_slabs(k):
    kb = k.astype(BF16)
    z = jnp.zeros_like(kb)
    return jnp.stack([jnp.concatenate([kb, z], -1), jnp.concatenate([z, kb], -1)], axis=1)


def _mixers_and_peer(x, rows, keys, qc, kc, kidx, k_sel, w, tq, tm, te, groups):
    lat_k, kpes_k, bk_k, bv_k, ik_k = keys
    (_, _, _, _, _, qa, qb, iq, iw, ga, gb) = rows
    b = lat_k.shape[0]
    ka, va = _kv_prep(lat_k, kpes_k, w)
    ik2, bk2, bv2 = _pair_slabs(ik_k), _pair_slabs(bk_k), _pair_slabs(bv_k)
    oa, ob = [], []
    for q0, nqg, s in groups:
        o = _mla_attn(qa, ka, va, qc, kc, tq, q0, nqg, s)
        oa.append(o.reshape(b, nqg * tq, o.shape[-1]))
        o = _dsa_attn(iq, iw, qb, ik2, bk2, bv2, qc, kc, kidx, tq, k_sel, q0, nqg, s)
        ob.append(o.reshape(b, nqg * tq, o.shape[-1]))
    oa = jnp.concatenate(oa, axis=1).reshape(x.shape[0], -1)
    ob = jnp.concatenate(ob, axis=1).reshape(x.shape[0], -1)
    y1, t2 = _out_proj(x, oa, ob, ga, gb, w, tm)
    tp = 2 * tm if x.shape[0] % (2 * tm) == 0 else tm
    return _peer(t2, y1, w, tp, te)


def kernel(x_prompt, x_sample, cache_a_latent, cache_a_kpe, cache_b_k, cache_b_v, cache_b_idx_k,
           meta_tokens, ln1_g, w_in, a_q_norm_g, a_kv_norm_g, a_w_uq, a_w_ukv, a_qk_g, b_qk_g,
           w_o_a, w_o_b, w_out, ln2_g, peer_wq, peer_subkeys, peer_u, peer_v):
    bp, sp, d = x_prompt.shape
    bs, ts, _ = x_sample.shape
    depth, _, past, _ = cache_a_latent.shape
    assert depth == 1 and sp % CHUNK == 0
    w = _prep_weights(ln1_g[0], w_in[0], a_q_norm_g[0], a_kv_norm_g[0], a_w_uq[0], a_w_ukv[0],
                      a_qk_g[0], b_qk_g[0], w_o_a[0], w_o_b[0], w_out[0], ln2_g[0], peer_wq[0],
                      peer_subkeys[0], peer_u[0], peer_v[0])
    tm = 256
    te = 512
    i32 = jnp.int32

    xq = x_prompt.reshape(bp * sp, d)
    rows_p = _in_proj(xq, _all_tables(N_META + jnp.arange(sp, dtype=i32)), bp, tm, w)
    rows_m = _in_proj(meta_tokens.astype(F32), _all_tables(jnp.arange(N_META, dtype=i32)), 1, N_META, w)
    xs = x_sample.reshape(bs * ts, d)
    pos_s = past + jnp.arange(ts, dtype=i32)
    rows_s = _in_proj(xs, _all_tables(jnp.tile(pos_s, bs)), 1, bs * ts, w)

    s_p = LANES + sp
    per_b = lambda a, b, t: a.reshape(b, t, a.shape[-1])
    meta_b = lambda a: jnp.broadcast_to(a[None], (bp,) + a.shape)
    keys_p = tuple(_key_layout(meta_b(m), per_b(r, bp, sp)) for r, m in zip(rows_p[:5], rows_m[:5]))
    chunk_q = jnp.arange(sp, dtype=i32) // CHUNK
    n_pad = LANES - N_META
    kc_p = jnp.concatenate([jnp.full((N_META,), -1, i32), jnp.full((n_pad,), PAD_CHUNK, i32),
                            chunk_q])[None, :]
    kidx_p = jnp.concatenate([jnp.arange(N_META, dtype=i32), s_p + jnp.arange(n_pad, dtype=i32),
                              N_META + jnp.arange(sp, dtype=i32)])[None, :]
    tq = 256
    nq = sp // tq
    tpg = max(1, nq // PROMPT_GROUPS)
    groups_p = [(q0, min(tpg, nq - q0), LANES + min(q0 + tpg, nq) * tq) for q0 in range(0, nq, tpg)]
    y_p = _mixers_and_peer(xq, rows_p, keys_p, chunk_q[:, None], kc_p, kidx_p,
                           min(IDX_TOPK, sp // 4), w, tq, tm, te, groups_p)

    s_s = _round_up(past + ts, LANES)
    kpes_cache = jnp.pad(cache_a_kpe[0], ((0, 0), (0, 0), (A_NOPE, LANES - A_NOPE - A_ROPE)))
    caches = (cache_a_latent[0], kpes_cache, cache_b_k[0], cache_b_v[0], cache_b_idx_k[0])
    keys_s = tuple(jnp.pad(jnp.concatenate([c.astype(F32), per_b(r, bs, ts)], axis=1),
                           ((0, 0), (0, s_s - past - ts), (0, 0)))
                   for c, r in zip(caches, rows_s[:5]))
    kc_s = jnp.concatenate([jnp.zeros((past,), i32), jnp.ones((ts,), i32),
                            jnp.full((s_s - past - ts,), PAD_CHUNK, i32)])[None, :]
    kidx_s = jnp.arange(s_s, dtype=i32)[None, :]
    y_s = _mixers_and_peer(xs, rows_s, keys_s, jnp.ones((ts, 1), i32), kc_s, kidx_s,
                           min(IDX_TOPK, (past + ts) // 4), w, ts, bs * ts, te, [(0, 1, s_s)])

    kpe_of = lambda r: r[:, A_NOPE:A_NOPE + A_ROPE]

    def new_rows_p(r, m):
        return jnp.concatenate([meta_b(m), per_b(r, bp, sp)], axis=1)[None]

    outs_p = [new_rows_p(rows_p[0], rows_m[0]), new_rows_p(kpe_of(rows_p[1]), kpe_of(rows_m[1]))]
    outs_p += [new_rows_p(rows_p[i], rows_m[i]) for i in (2, 3, 4)]
    outs_s = [per_b(rows_s[0], bs, ts)[None], per_b(kpe_of(rows_s[1]), bs, ts)[None]]
    outs_s += [per_b(rows_s[i], bs, ts)[None] for i in (2, 3, 4)]
    return (y_p.reshape(bp, sp, d), y_s.reshape(bs, ts, d), *outs_p, *outs_s)
```

```python
import functools

import jax
import jax.numpy as jnp
import numpy as np
from jax import lax
from jax.experimental import pallas as pl
from jax.experimental.pallas import tpu as pltpu

F32 = jnp.float32
BF16 = jnp.bfloat16

LANES = 128
VMEM_LIMIT = 52 * 1024 * 1024

CHUNK = 64
N_META = 16
ROPE_THETA = 500000.0
EPS = 1e-6

A_HEADS = 8
A_NOPE = 64
A_ROPE = 32
A_V = 64
A_QRANK = 256
A_KVRANK = 128
A_SCALE = (A_NOPE + A_ROPE) ** -0.5

B_HEADS = 8
B_HEAD_DIM = 64
B_ROT = B_HEAD_DIM // 4
B_SCALE = B_HEAD_DIM ** -0.5
IDX_HEADS = 8
IDX_DIM = 64
IDX_ROT = IDX_DIM // 4
IDX_TOPK = 256
IDX_W_SCALE = (IDX_HEADS * IDX_DIM) ** -0.5

PEER_HEADS = 8
PEER_DKEY = 128
N_KEYS = 128
PEER_TOPK = 16
PROMPT_GROUPS = 16

IN_SPLITS = (A_QRANK, A_KVRANK, A_ROPE, B_HEADS * B_HEAD_DIM, B_HEAD_DIM, B_HEAD_DIM,
             IDX_HEADS * IDX_DIM, IDX_HEADS, IDX_DIM)

NEG_BIG = -1e30
LOG2_E = 1.4426950408889634
INT_MIN = -2147483648
PAD_CHUNK = 1 << 30

NT_DIMS = (((1,), (1,)), ((), ()))


def _cparams(*sem):
    return pltpu.CompilerParams(dimension_semantics=sem, vmem_limit_bytes=VMEM_LIMIT)


def _round_up(n, m):
    return -(-n // m) * m


def _pick_tile(n, target):
    best = LANES
    for t in range(LANES, min(n, target) + 1, LANES):
        if n % t == 0:
            best = t
    return best


def _rot_tables(pos, rot_dim, offsets):
    half = rot_dim // 2
    inv = ROPE_THETA ** (-jnp.arange(half, dtype=F32) * (2.0 / rot_dim))
    freq_idx = np.zeros((LANES,), np.int32)
    first = np.zeros((LANES,), bool)
    second = np.zeros((LANES,), bool)
    for o in offsets:
        freq_idx[o:o + rot_dim] = np.arange(rot_dim) % half
        first[o:o + half] = True
        second[o + half:o + rot_dim] = True
    ang = pos.astype(F32)[:, None] * inv[freq_idx][None, :]
    cos, sin = jnp.cos(ang), jnp.sin(ang)
    c = jnp.where((first | second)[None, :], cos, 1.0)
    sm = jnp.where(first[None, :], -sin, 0.0)
    sp = jnp.where(second[None, :], sin, 0.0)
    return jnp.stack([c, sm, sp])


def _all_tables(pos):
    return jnp.concatenate([
        _rot_tables(pos, A_ROPE, (A_NOPE,)),
        _rot_tables(pos, B_ROT, (0, B_HEAD_DIM)),
        _rot_tables(pos, B_ROT, (0,)),
    ])


def _rot(x, tab_ref, base, half):
    c = tab_ref[base]
    sm = tab_ref[base + 1]
    sp = tab_ref[base + 2]
    return x * c + pltpu.roll(x, LANES - half, 1) * sm + pltpu.roll(x, half, 1) * sp


SLAB_CQ, SLAB_CKV, SLAB_KPE, SLAB_BQ, SLAB_BK, SLAB_BV = 0, 2, 3, 4, 8, 9
SLAB_IQ, SLAB_IW, SLAB_IK, SLAB_GA, N_SLAB_SMALL = 10, 14, 15, 16, 16


def _cols(s0, n=1):
    return slice(s0 * LANES, (s0 + n) * LANES)


def _inproj_kernel(x_ref, ln1_ref, win_ref, wg_ref, gq_ref, gkv_ref, wuq_ref, gs_ref, tab_ref,
                   lat_ref, kpes_ref, bk_ref, bv_ref, ik_ref, qa_ref, qb_ref, iq_ref, iw_ref,
                   ga_ref, gb_ref):
    d = x_ref.shape[1]
    x = x_ref[...]
    h = x * lax.rsqrt(jnp.mean(x * x, axis=-1, keepdims=True) + EPS) * ln1_ref[...]
    hb = h.astype(BF16)
    lane = lax.broadcasted_iota(jnp.int32, (1, LANES), 1)
    lo = lane < B_HEAD_DIM

    def proj(s0, n=1):
        return jnp.dot(hb, win_ref[:, _cols(s0, n)], preferred_element_type=F32)

    def seg_rs(x2, mask, n):
        return lax.rsqrt(jnp.sum(jnp.where(mask, x2, 0.0), axis=-1, keepdims=True) * (1.0 / n) + EPS)

    cq = proj(SLAB_CQ, 2)
    cq = cq * lax.rsqrt(jnp.mean(cq * cq, axis=-1, keepdims=True) + EPS) * gq_ref[...]
    q = jnp.dot(cq.astype(BF16), wuq_ref[...], preferred_element_type=F32)
    rope_m = jnp.logical_and(lane >= A_NOPE, lane < A_NOPE + A_ROPE)
    for hd in range(A_HEADS):
        s = q[:, _cols(hd)]
        s2 = s * s
        sc = jnp.where(lo, seg_rs(s2, lo, A_NOPE), seg_rs(s2, rope_m, A_ROPE))
        qa_ref[:, _cols(hd)] = _rot(s * sc * gs_ref[0:1, :], tab_ref, 0, A_ROPE // 2).astype(BF16)

    ckv = proj(SLAB_CKV)
    lat_ref[...] = ckv * lax.rsqrt(jnp.mean(ckv * ckv, axis=-1, keepdims=True) + EPS) * gkv_ref[...]
    kp = proj(SLAB_KPE)
    kp = kp * seg_rs(kp * kp, rope_m, A_ROPE) * gs_ref[1:2, :]
    kpes_ref[...] = _rot(kp, tab_ref, 0, A_ROPE // 2)

    for j in range(B_HEADS // 2):
        s = proj(SLAB_BQ + j)
        s2 = s * s
        sc = jnp.where(lo, seg_rs(s2, lo, B_HEAD_DIM), seg_rs(s2, jnp.logical_not(lo), B_HEAD_DIM))
        s = _rot(s * sc * gs_ref[2:3, :], tab_ref, 3, B_ROT // 2) * B_SCALE
        qb_ref[:, _cols(j)] = s.astype(BF16)
    s = proj(SLAB_BK)
    s = s * seg_rs(s * s, lo, B_HEAD_DIM) * gs_ref[3:4, :]
    bk_ref[...] = _rot(s, tab_ref, 6, B_ROT // 2)[:, :B_HEAD_DIM]
    bv_ref[...] = proj(SLAB_BV)[:, :B_HEAD_DIM]

    for j in range(IDX_HEADS // 2):
        iq_ref[:, _cols(j)] = _rot(proj(SLAB_IQ + j), tab_ref, 3, IDX_ROT // 2).astype(BF16)
    iw_ref[...] = proj(SLAB_IW) * IDX_W_SCALE
    ik_ref[...] = _rot(proj(SLAB_IK), tab_ref, 6, IDX_ROT // 2)[:, :IDX_DIM]

    nd = d // LANES
    for j in range(nd):
        ga_ref[:, _cols(j)] = jax.nn.sigmoid(
            jnp.dot(hb, wg_ref[:, _cols(j)], preferred_element_type=F32))
        gb_ref[:, _cols(j)] = jax.nn.sigmoid(
            jnp.dot(hb, wg_ref[:, _cols(nd + j)], preferred_element_type=F32))


def _in_proj(x, tabs, reps, tm, w):
    n, d = x.shape
    r = tabs.shape[1]
    npb = r // tm
    assert r % tm == 0 and n == reps * r
    tok = lambda width: pl.BlockSpec((tm, width), lambda p, b: (b * npb + p, 0))
    full = lambda a: pl.BlockSpec(a.shape, lambda p, b: (0,) * a.ndim)
    outs = [
        ((n, A_KVRANK), F32), ((n, LANES), F32), ((n, B_HEAD_DIM), F32), ((n, B_HEAD_DIM), F32),
        ((n, IDX_DIM), F32), ((n, A_HEADS * LANES), BF16), ((n, B_HEADS * B_HEAD_DIM), BF16),
        ((n, IDX_HEADS * IDX_DIM), BF16), ((n, LANES), F32), ((n, d), F32), ((n, d), F32),
    ]
    return pl.pallas_call(
        _inproj_kernel,
        grid=(npb, reps),
        in_specs=[tok(d), full(w['ln1']), full(w['win']), full(w['wg']), full(w['gq']), full(w['gkv']),
                  full(w['wuq']), full(w['gs']),
                  pl.BlockSpec((tabs.shape[0], tm, LANES), lambda p, b: (0, p, 0))],
        out_specs=[tok(s[1]) for s, _ in outs],
        out_shape=[jax.ShapeDtypeStruct(s, t) for s, t in outs],
        compiler_params=_cparams("arbitrary", "arbitrary"),
        name="in_proj",
    )(x, w['ln1'], w['win'], w['wg'], w['gq'], w['gkv'], w['wuq'], w['gs'], tabs)


def _kvprep_kernel(lat_ref, kpes_ref, wk_ref, wv_ref, gk_ref, ka_ref, va_ref):
    lat = lat_ref[...].astype(BF16)
    kk = jnp.dot(lat, wk_ref[...], preferred_element_type=F32)
    vv = jnp.dot(lat, wv_ref[...], preferred_element_type=F32)
    kpes = kpes_ref[...]
    for hd in range(A_HEADS):
        s = kk[:, _cols(hd)]
        rs = lax.rsqrt(jnp.sum(s * s, axis=-1, keepdims=True) * (1.0 / A_NOPE) + EPS)
        ka_ref[hd] = (s * rs * gk_ref[...] + kpes).astype(BF16)
        va_ref[hd] = vv[:, _cols(hd)].astype(BF16)


def _kv_prep(lat_keys, kpes_keys, w):
    b, s, _ = lat_keys.shape
    ts = _pick_tile(s, 512)
    row = pl.BlockSpec((None, ts, LANES), lambda i, j: (i, j, 0))
    full = lambda a: pl.BlockSpec(a.shape, lambda i, j: (0,) * a.ndim)
    hd = pl.BlockSpec((None, A_HEADS, ts, LANES), lambda i, j: (i, 0, j, 0))
    shp = jax.ShapeDtypeStruct((b, A_HEADS, s, LANES), BF16)
    return pl.pallas_call(
        _kvprep_kernel,
        grid=(b, s // ts),
        in_specs=[row, row, full(w['wk']), full(w['wv']), full(w['gk'])],
        out_specs=[hd, hd],
        out_shape=[shp, shp],
        compiler_params=_cparams("arbitrary", "arbitrary"),
        name="kv_prep",
    )(lat_keys, kpes_keys, w['wk'], w['wv'], w['gk'])


def _mla_kernel(q_ref, k_ref, v_ref, qc_ref, kc_ref, o_ref, bias_s):
    j = pl.program_id(2)

    @pl.when(pl.program_id(1) == 0)
    def _():
        bias_s[j] = jnp.where(kc_ref[...] <= qc_ref[...], 0.0, NEG_BIG)

    s = lax.dot_general(q_ref[...], k_ref[...], NT_DIMS, preferred_element_type=F32) + bias_s[j]
    p = jnp.exp2((s - jnp.max(s, axis=-1, keepdims=True)) * (A_SCALE * LOG2_E))
    inv_l = 1.0 / jnp.sum(p, axis=-1, keepdims=True)
    o = jnp.dot(p.astype(BF16), v_ref[...], preferred_element_type=F32)
    o_ref[...] = (o * inv_l).astype(BF16)


def _mla_attn(qa, ka, va, qc, kc, tq, q0, nqg, s):
    n = qa.shape[0]
    b = ka.shape[0]
    nq = n // (b * tq)
    qspec = pl.BlockSpec((tq, LANES), lambda i, h, j: (i * nq + q0 + j, h))
    kspec = pl.BlockSpec((None, None, s, LANES), lambda i, h, j: (i, h, 0, 0))
    return pl.pallas_call(
        _mla_kernel,
        grid=(b, A_HEADS, nqg),
        in_specs=[qspec, kspec, kspec,
                  pl.BlockSpec((tq, 1), lambda i, h, j: (q0 + j, 0)),
                  pl.BlockSpec((1, s), lambda i, h, j: (0, 0))],
        out_specs=pl.BlockSpec((tq, LANES), lambda i, h, j: (i * nqg + j, h)),
        out_shape=jax.ShapeDtypeStruct((b * nqg * tq, A_HEADS * LANES), BF16),
        scratch_shapes=[pltpu.VMEM((nqg, tq, s), F32)],
        compiler_params=_cparams("arbitrary", "arbitrary", "arbitrary"),
        name="mla_attn",
    )(qa, ka, va, qc, kc)


def _dsa_kernel(iq_ref, iw_ref, qb_ref, ik2_ref, bk2_ref, bv2_ref, qc_ref, kc_ref, kidx_ref,
                o_ref, key_s, bias_s, *, k_sel, idx_bits):
    tq = key_s.shape[0]
    adm = kc_ref[...] <= qc_ref[...]
    iw = iw_ref[...]
    score = None
    for j in range(IDX_HEADS // 2):
        iqj = iq_ref[:, _cols(j)]
        for half in range(2):
            hd = 2 * j + half
            rel = jnp.maximum(
                lax.dot_general(iqj, ik2_ref[half], NT_DIMS, preferred_element_type=F32), 0.0)
            term = iw[:, hd:hd + 1] * rel
            score = term if score is None else score + term

    bits = lax.bitcast_convert_type(score, jnp.int32)
    key = jnp.where(bits < 0, bits ^ jnp.int32(0x7FFFFFFF), bits)
    key = jnp.where(key == -1, 0, key)
    key_s[...] = jnp.where(adm, key, INT_MIN)

    kf = float(k_sel)

    def count(mask):
        return jnp.sum(jnp.where(mask, 1.0, 0.0), axis=-1, keepdims=True)

    def value_step(i, cur):
        cand = cur + lax.shift_left(jnp.int32(1), 31 - i)
        return jnp.where(count(key_s[...] >= cand) >= kf, cand, cur)

    tau = lax.fori_loop(0, 32, value_step, jnp.full((tq, 1), INT_MIN, jnp.int32))
    key = key_s[...]
    ge = key >= tau
    c_ge = count(ge)
    bias_s[...] = jnp.where(jnp.logical_and(ge, adm), 0.0, NEG_BIG)
    amb = jnp.logical_and(tau > INT_MIN, c_ge > kf)

    @pl.when(jnp.max(jnp.where(amb, 1.0, 0.0)) > 0.0)
    def _():
        key = key_s[...]
        gt = key > tau
        tie = key == tau
        need = kf - count(gt)
        kidx = kidx_ref[...]

        def index_step(i, cur):
            cand = cur + lax.shift_left(jnp.int32(1), idx_bits - 1 - i)
            below = count(jnp.logical_and(tie, kidx < cand))
            return jnp.where(below < need, cand, cur)

        last = lax.fori_loop(0, idx_bits, index_step, jnp.zeros((tq, 1), jnp.int32))
        sel = jnp.logical_or(gt, jnp.logical_and(tie, kidx <= last))
        bias_s[...] = jnp.where(jnp.logical_and(sel, adm), 0.0, NEG_BIG)

    for j in range(B_HEADS // 2):
        qbj = qb_ref[:, _cols(j)]
        acc = None
        for half in range(2):
            lg = lax.dot_general(qbj, bk2_ref[half], NT_DIMS, preferred_element_type=F32)
            lg = lg + bias_s[...]
            p = jnp.exp(lg - jnp.max(lg, axis=-1, keepdims=True))
            inv_l = 1.0 / jnp.sum(p, axis=-1, keepdims=True)
            o = jnp.dot(p.astype(BF16), bv2_ref[half], preferred_element_type=F32) * inv_l
            acc = o if acc is None else acc + o
        o_ref[:, _cols(j)] = acc.astype(BF16)


def _dsa_attn(iq, iw, qb, ik2, bk2, bv2, qc, kc, kidx, tq, k_sel, q0, nqg, s):
    n = iq.shape[0]
    b, _, s_all, _ = ik2.shape
    nq = n // (b * tq)
    idx_bits = max(1, int(np.ceil(np.log2(s_all + N_META + 1))))
    tok = lambda width: pl.BlockSpec((tq, width), lambda i, j: (i * nq + q0 + j, 0))
    kspec = pl.BlockSpec((None, 2, s, LANES), lambda i, j: (i, 0, 0, 0))
    row = pl.BlockSpec((1, s), lambda i, j: (0, 0))
    return pl.pallas_call(
        functools.partial(_dsa_kernel, k_sel=k_sel, idx_bits=idx_bits),
        grid=(b, nqg),
        in_specs=[tok(iq.shape[1]), tok(LANES), tok(qb.shape[1]), kspec, kspec, kspec,
                  pl.BlockSpec((tq, 1), lambda i, j: (q0 + j, 0)), row, row],
        out_specs=pl.BlockSpec((tq, qb.shape[1]), lambda i, j: (i * nqg + j, 0)),
        out_shape=jax.ShapeDtypeStruct((b * nqg * tq, qb.shape[1]), BF16),
        scratch_shapes=[pltpu.VMEM((tq, s), jnp.int32), pltpu.VMEM((tq, s), F32)],
        compiler_params=_cparams("arbitrary", "arbitrary"),
        name="dsa_attn",
    )(iq, iw, qb, ik2, bk2, bv2, qc, kc, kidx)


def _outproj_kernel(x_ref, oa_ref, ob_ref, ga_ref, gb_ref, woa_ref, wob_ref, wout_ref, ln2_ref,
                    y_ref, t_ref):
    ya = jnp.dot(oa_ref[...], woa_ref[...], preferred_element_type=F32)
    yb = jnp.dot(ob_ref[...], wob_ref[...], preferred_element_type=F32)
    mg = ga_ref[...] * ya + gb_ref[...] * yb
    y = x_ref[...] + jnp.dot(mg.astype(BF16), wout_ref[...], preferred_element_type=F32)
    y_ref[...] = y
    t = y * lax.rsqrt(jnp.mean(y * y, axis=-1, keepdims=True) + EPS) * ln2_ref[...]
    t_ref[...] = t.astype(BF16)


def _out_proj(x, oa, ob, ga, gb, w, tm):
    n, d = x.shape
    tok = lambda a: pl.BlockSpec((tm, a.shape[1]), lambda i: (i, 0))
    full = lambda a: pl.BlockSpec(a.shape, lambda i: (0,) * a.ndim)
    return pl.pallas_call(
        _outproj_kernel,
        grid=(n // tm,),
        in_specs=[tok(x), tok(oa), tok(ob), tok(ga), tok(gb),
                  full(w['woa']), full(w['wob']), full(w['wout']), full(w['ln2'])],
        out_specs=[tok(x), tok(x)],
        out_shape=[jax.ShapeDtypeStruct((n, d), F32), jax.ShapeDtypeStruct((n, d), BF16)],
        compiler_params=_cparams("arbitrary"),
        name="out_proj",
    )(x, oa, ob, ga, gb, w['woa'], w['wob'], w['wout'], w['ln2'])


def _top_rows(s, k, with_rank=False):
    t = s.shape[1]
    row = lax.broadcasted_iota(jnp.int32, (k, t), 0).astype(F32)
    out = jnp.zeros((k, t), F32)
    seen = jnp.zeros((1, t), F32)
    rank = jnp.full(s.shape, float(k), F32)
    for _ in range(k):
        m = jnp.max(s, axis=0, keepdims=True)
        eq = s == m
        if with_rank:
            rank = jnp.where(eq, seen, rank)
        upto = seen + jnp.sum(jnp.where(eq, 1.0, 0.0), axis=0, keepdims=True)
        out = jnp.where(jnp.logical_and(row >= seen, row < upto), m, out)
        seen = upto
        s = jnp.where(eq, -jnp.inf, s)
    return (out, rank) if with_rank else out


def _top_rows_distinct(s, k, with_rank):
    t = s.shape[1]
    row = lax.broadcasted_iota(jnp.int32, (k, t), 0)
    out = jnp.zeros((k, t), F32)
    rank = jnp.full(s.shape, float(k), F32)
    for q in range(k):
        m = jnp.max(s, axis=0, keepdims=True)
        eq = s == m
        if with_rank:
            rank = jnp.where(eq, float(q), rank)
        out = jnp.where(row == q, m, out)
        s = jnp.where(eq, -jnp.inf, s)
    used = jnp.sum(jnp.where(s == -jnp.inf, 1.0, 0.0), axis=0, keepdims=True)
    return out, rank, used


def _top_rows_into(s, k, n_inf, out_s, rank_s=None):
    out, rank, used = _top_rows_distinct(s, k, rank_s is not None)
    out_s[...] = out
    if rank_s is not None:
        rank_s[...] = rank

    @pl.when(jnp.max(used) > float(k + n_inf))
    def _():
        if rank_s is None:
            out_s[...] = _top_rows(s, k)
        else:
            out_s[...], rank_s[...] = _top_rows(s, k, with_rank=True)


def _pair_sum_candidates(a, b):
    k = PEER_TOPK
    assert k == 16 and a.shape[0] == k
    row = lax.broadcasted_iota(jnp.int32, (8, a.shape[1]), 0)
    blocks = [a[0:1] + b]
    n_inf = 0
    for i in range(1, 8):
        blocks.append(jnp.where(row < k // (i + 1), a[i:i + 1] + b[:8], -jnp.inf))
        n_inf += 8 - k // (i + 1)
    blocks.append(a[8:] + b[0:1])
    return jnp.concatenate(blocks, axis=0), n_inf


def _peer_kernel(t_ref, y_ref, wqt_ref, sk1_ref, sk2_ref, u_ref, vt_ref, o_ref,
                 m_s, r_s, a_s, b_s, gw_s, acc_s, qt_s, t1_s, t2_s, tp_s, rk_s):
    c = pl.program_id(1)
    nc = pl.num_programs(1)
    te = u_ref.shape[0]
    half = PEER_DKEY // 2

    @pl.when(c == 0)
    def _route():
        qt = lax.dot_general(wqt_ref[...], t_ref[...], NT_DIMS, preferred_element_type=F32)
        qt_s[...] = qt.astype(BF16)

        @pl.loop(0, PEER_HEADS)
        def _head(hd):
            row0 = pl.multiple_of(hd * PEER_DKEY, PEER_DKEY)
            q1 = qt_s[pl.ds(row0, half), :]
            q2 = qt_s[pl.ds(row0 + half, half), :]
            s1 = jnp.dot(sk1_ref[hd], q1, preferred_element_type=F32)
            s2 = jnp.dot(sk2_ref[hd], q2, preferred_element_type=F32)
            _top_rows_into(s1, PEER_TOPK, 0, t1_s)
            _top_rows_into(s2, PEER_TOPK, 0, t2_s, rk_s)
            top1, top2, r = t1_s[...], t2_s[...], rk_s[...]
            cand, n_inf = _pair_sum_candidates(top1, top2)
            _top_rows_into(cand, PEER_TOPK, n_inf, tp_s)
            top = tp_s[...]
            tau = top[PEER_TOPK - 1:PEER_TOPK]
            z = jnp.sum(jnp.exp(top - top[0:1]), axis=0, keepdims=True)
            m = jnp.zeros(s1.shape, F32)
            for j in range(PEER_TOPK):
                reach = top1 + top2[j:j + 1] >= tau
                c_j = jnp.min(jnp.where(reach, top1, jnp.inf), axis=0, keepdims=True)
                m = m + jnp.where(s1 >= c_j, 1.0, 0.0)
            m_s[hd] = m
            r_s[hd] = r.astype(BF16)
            a_s[hd] = jnp.exp(s1 - top1[0:1])
            b_s[hd] = (jnp.exp(s2 - top2[0:1]) * (1.0 / z)).astype(BF16)
        acc_s[...] = jnp.zeros_like(acc_s)

    at = lax.dot_general(u_ref[...], t_ref[...], NT_DIMS, preferred_element_type=F32)
    zero = jnp.zeros((), BF16)
    for k in range(te // N_KEYS):
        i1 = c * (te // N_KEYS) + k
        w = None
        for hd in range(PEER_HEADS):
            sel = r_s[hd] < m_s[hd, pl.ds(i1, 1), :].astype(BF16)
            term = jnp.where(sel, b_s[hd] * a_s[hd, pl.ds(i1, 1), :].astype(BF16), zero)
            w = term if w is None else w + term
        g = jax.nn.gelu(at[k * N_KEYS:(k + 1) * N_KEYS])
        gw_s[k * N_KEYS:(k + 1) * N_KEYS, :] = w * g.astype(BF16)
    acc_s[...] += jnp.dot(vt_ref[...], gw_s[...], preferred_element_type=F32)

    @pl.when(c == nc - 1)
    def _finish():
        o_ref[...] = y_ref[...] + acc_s[...].T


def _peer(t, y, w, tm, te):
    n, d = y.shape
    nc = w['u'].shape[0] // te
    full = lambda a: pl.BlockSpec(a.shape, lambda i, c: (0,) * a.ndim)
    tok = pl.BlockSpec((tm, d), lambda i, c: (i, 0))
    hk = (PEER_HEADS, N_KEYS, tm)
    return pl.pallas_call(
        _peer_kernel,
        grid=(n // tm, nc),
        in_specs=[tok, tok, full(w['wqt']), full(w['sk1']), full(w['sk2']),
                  pl.BlockSpec((te, d), lambda i, c: (c, 0)),
                  pl.BlockSpec((d, te), lambda i, c: (0, c))],
        out_specs=tok,
        out_shape=jax.ShapeDtypeStruct((n, d), F32),
        scratch_shapes=[pltpu.VMEM(hk, F32), pltpu.VMEM(hk, BF16), pltpu.VMEM(hk, F32),
                        pltpu.VMEM(hk, BF16), pltpu.VMEM((te, tm), BF16), pltpu.VMEM((d, tm), F32),
                        pltpu.VMEM((PEER_HEADS * PEER_DKEY, tm), BF16),
                        pltpu.VMEM((PEER_TOPK, tm), F32), pltpu.VMEM((PEER_TOPK, tm), F32),
                        pltpu.VMEM((PEER_TOPK, tm), F32), pltpu.VMEM((N_KEYS, tm), F32)],
        compiler_params=_cparams("arbitrary", "arbitrary"),
        name="peer",
    )(t, y, w['wqt'], w['sk1'], w['sk2'], w['u'], w['vt'])


def _prep_weights(ln1_g, w_in, a_q_norm_g, a_kv_norm_g, a_w_uq, a_w_ukv, a_qk_g, b_qk_g,
                  w_o_a, w_o_b, w_out, ln2_g, peer_wq, peer_subkeys, peer_u, peer_v):
    d = w_in.shape[0]
    offs = np.cumsum(IN_SPLITS).tolist()
    cq, ckv, kpe, bq, bk, bv, iq, iw, ik, gates = jnp.split(w_in, offs, axis=1)
    z = lambda n: jnp.zeros((d, n), w_in.dtype)
    win = jnp.concatenate([
        cq, ckv, z(A_NOPE), kpe, z(LANES - A_NOPE - A_ROPE), bq,
        bk, z(LANES - B_HEAD_DIM), bv, z(LANES - B_HEAD_DIM), iq,
        iw, z(LANES - IDX_HEADS), ik, z(LANES - IDX_DIM)], axis=1).astype(BF16)
    assert win.shape[1] == N_SLAB_SMALL * LANES

    def pad_last(a, n):
        return jnp.pad(a, [(0, 0)] * (a.ndim - 1) + [(0, n - a.shape[-1])])

    wuq = pad_last(a_w_uq.reshape(A_QRANK, A_HEADS, A_NOPE + A_ROPE), LANES)
    wuq = wuq.reshape(A_QRANK, A_HEADS * LANES).astype(BF16)
    ukv = a_w_ukv.reshape(A_KVRANK, A_HEADS, A_NOPE + A_V)
    wk = pad_last(ukv[..., :A_NOPE], LANES).reshape(A_KVRANK, A_HEADS * LANES).astype(BF16)
    wv = pad_last(ukv[..., A_NOPE:], LANES).reshape(A_KVRANK, A_HEADS * LANES).astype(BF16)
    woa = pad_last(w_o_a.reshape(A_HEADS, A_V, d).transpose(0, 2, 1), LANES)
    woa = woa.transpose(0, 2, 1).reshape(A_HEADS * LANES, d).astype(BF16)

    row = lambda v: pad_last(v, LANES)[None, :]
    gs = jnp.concatenate([
        row(a_qk_g[0]),
        row(jnp.concatenate([jnp.zeros((A_NOPE,), F32), a_qk_g[1, A_NOPE:]])),
        row(jnp.concatenate([b_qk_g[0], b_qk_g[0]])),
        row(b_qk_g[1]),
        jnp.zeros((4, LANES), F32)], axis=0)
    return dict(
        ln1=ln1_g[None, :], win=win, wg=gates.astype(BF16), gq=a_q_norm_g[None, :],
        gkv=a_kv_norm_g[None, :], wuq=wuq, gs=gs, wk=wk, wv=wv, gk=row(a_qk_g[1, :A_NOPE]),
        woa=woa, wob=w_o_b.astype(BF16), wout=w_out.astype(BF16), ln2=ln2_g[None, :],
        wqt=peer_wq.T.astype(BF16),
        sk1=peer_subkeys[:, 0].astype(BF16), sk2=peer_subkeys[:, 1].astype(BF16),
        u=peer_u.astype(BF16), vt=peer_v.T.astype(BF16))


def _key_layout(meta, rows):
    pad = jnp.zeros((meta.shape[0], LANES - meta.shape[1], meta.shape[2]), meta.dtype)
    return jnp.concatenate([meta, pad, rows], axis=1)


def _pair_slabs(k):
    kb = k.astype(BF16)
    z = jnp.zeros_like(kb)
    return jnp.stack([jnp.concatenate([kb, z], -1), jnp.concatenate([z, kb], -1)], axis=1)


def _mixers_and_peer(x, rows, keys, qc, kc, kidx, k_sel, w, tq, tm, te, groups):
    lat_k, kpes_k, bk_k, bv_k, ik_k = keys
    (_, _, _, _, _, qa, qb, iq, iw, ga, gb) = rows
    b = lat_k.shape[0]
    ka, va = _kv_prep(lat_k, kpes_k, w)
    ik2, bk2, bv2 = _pair_slabs(ik_k), _pair_slabs(bk_k), _pair_slabs(bv_k)
    oa, ob = [], []
    for q0, nqg, s in groups:
        o = _mla_attn(qa, ka, va, qc, kc, tq, q0, nqg, s)
        oa.append(o.reshape(b, nqg * tq, o.shape[-1]))
        o = _dsa_attn(iq, iw, qb, ik2, bk2, bv2, qc, kc, kidx, tq, k_sel, q0, nqg, s)
        ob.append(o.reshape(b, nqg * tq, o.shape[-1]))
    oa = jnp.concatenate(oa, axis=1).reshape(x.shape[0], -1)
    ob = jnp.concatenate(ob, axis=1).reshape(x.shape[0], -1)
    y1, t2 = _out_proj(x, oa, ob, ga, gb, w, tm)
    tp = 2 * tm if x.shape[0] % (2 * tm) == 0 else tm
    return _peer(t2, y1, w, tp, te)


def kernel(x_prompt, x_sample, cache_a_latent, cache_a_kpe, cache_b_k, cache_b_v, cache_b_idx_k,
           meta_tokens, ln1_g, w_in, a_q_norm_g, a_kv_norm_g, a_w_uq, a_w_ukv, a_qk_g, b_qk_g,
           w_o_a, w_o_b, w_out, ln2_g, peer_wq, peer_subkeys, peer_u, peer_v):
    bp, sp, d = x_prompt.shape
    bs, ts, _ = x_sample.shape
    depth, _, past, _ = cache_a_latent.shape
    assert depth == 1 and sp % CHUNK == 0
    w = _prep_weights(ln1_g[0], w_in[0], a_q_norm_g[0], a_kv_norm_g[0], a_w_uq[0], a_w_ukv[0],
                      a_qk_g[0], b_qk_g[0], w_o_a[0], w_o_b[0], w_out[0], ln2_g[0], peer_wq[0],
                      peer_subkeys[0], peer_u[0], peer_v[0])
    tm = 256
    te = 512
    i32 = jnp.int32

    xq = x_prompt.reshape(bp * sp, d)
    rows_p = _in_proj(xq, _all_tables(N_META + jnp.arange(sp, dtype=i32)), bp, tm, w)
    rows_m = _in_proj(meta_tokens.astype(F32), _all_tables(jnp.arange(N_META, dtype=i32)), 1, N_META, w)
    xs = x_sample.reshape(bs * ts, d)
    pos_s = past + jnp.arange(ts, dtype=i32)
    rows_s = _in_proj(xs, _all_tables(jnp.tile(pos_s, bs)), 1, bs * ts, w)

    s_p = LANES + sp
    per_b = lambda a, b, t: a.reshape(b, t, a.shape[-1])
    meta_b = lambda a: jnp.broadcast_to(a[None], (bp,) + a.shape)
    keys_p = tuple(_key_layout(meta_b(m), per_b(r, bp, sp)) for r, m in zip(rows_p[:5], rows_m[:5]))
    chunk_q = jnp.arange(sp, dtype=i32) // CHUNK
    n_pad = LANES - N_META
    kc_p = jnp.concatenate([jnp.full((N_META,), -1, i32), jnp.full((n_pad,), PAD_CHUNK, i32),
                            chunk_q])[None, :]
    kidx_p = jnp.concatenate([jnp.arange(N_META, dtype=i32), s_p + jnp.arange(n_pad, dtype=i32),
                              N_META + jnp.arange(sp, dtype=i32)])[None, :]
    tq = 256
    nq = sp // tq
    tpg = max(1, nq // PROMPT_GROUPS)
    groups_p = [(q0, min(tpg, nq - q0), LANES + min(q0 + tpg, nq) * tq) for q0 in range(0, nq, tpg)]
    y_p = _mixers_and_peer(xq, rows_p, keys_p, chunk_q[:, None], kc_p, kidx_p,
                           min(IDX_TOPK, sp // 4), w, tq, tm, te, groups_p)

    s_s = _round_up(past + ts, LANES)
    kpes_cache = jnp.pad(cache_a_kpe[0], ((0, 0), (0, 0), (A_NOPE, LANES - A_NOPE - A_ROPE)))
    caches = (cache_a_latent[0], kpes_cache, cache_b_k[0], cache_b_v[0], cache_b_idx_k[0])
    keys_s = tuple(jnp.pad(jnp.concatenate([c.astype(F32), per_b(r, bs, ts)], axis=1),
                           ((0, 0), (0, s_s - past - ts), (0, 0)))
                   for c, r in zip(caches, rows_s[:5]))
    kc_s = jnp.concatenate([jnp.zeros((past,), i32), jnp.ones((ts,), i32),
                            jnp.full((s_s - past - ts,), PAD_CHUNK, i32)])[None, :]
    kidx_s = jnp.arange(s_s, dtype=i32)[None, :]
    y_s = _mixers_and_peer(xs, rows_s, keys_s, jnp.ones((ts, 1), i32), kc_s, kidx_s,
                           min(IDX_TOPK, (past + ts) // 4), w, ts, bs * ts, te, [(0, 1, s_s)])

    kpe_of = lambda r: r[:, A_NOPE:A_NOPE + A_ROPE]

    def new_rows_p(r, m):
        return jnp.concatenate([meta_b(m), per_b(r, bp, sp)], axis=1)[None]

    outs_p = [new_rows_p(rows_p[0], rows_m[0]), new_rows_p(kpe_of(rows_p[1]), kpe_of(rows_m[1]))]
    outs_p += [new_rows_p(rows_p[i], rows_m[i]) for i in (2, 3, 4)]
    outs_s = [per_b(rows_s[0], bs, ts)[None], per_b(kpe_of(rows_s[1]), bs, ts)[None]]
    outs_s += [per_b(rows_s[i], bs, ts)[None] for i in (2, 3, 4)]
    return (y_p.reshape(bp, sp, d), y_s.reshape(bs, ts, d), *outs_p, *outs_s)
```

```python
import functools

import jax
import jax.numpy as jnp
import numpy as np
from jax import lax
from jax.experimental import pallas as pl
from jax.experimental.pallas import tpu as pltpu

F32 = jnp.float32
BF16 = jnp.bfloat16

LANES = 128
BF16_ROWS = 16
VMEM_LIMIT = 52 * 1024 * 1024

CHUNK = 64
N_META = 16
ROPE_THETA = 500000.0
EPS = 1e-6

A_HEADS = 8
A_NOPE = 64
A_ROPE = 32
A_V = 64
A_QRANK = 256
A_KVRANK = 128
A_SCALE = (A_NOPE + A_ROPE) ** -0.5

B_HEADS = 8
B_HEAD_DIM = 64
B_ROT = B_HEAD_DIM // 4
B_SCALE = B_HEAD_DIM ** -0.5
IDX_HEADS = 8
IDX_DIM = 64
IDX_ROT = IDX_DIM // 4
IDX_TOPK = 256
IDX_W_SCALE = (IDX_HEADS * IDX_DIM) ** -0.5

PEER_HEADS = 8
PEER_DKEY = 128
N_KEYS = 128
PEER_TOPK = 16
PROMPT_GROUPS = 16

IN_SPLITS = (A_QRANK, A_KVRANK, A_ROPE, B_HEADS * B_HEAD_DIM, B_HEAD_DIM, B_HEAD_DIM,
             IDX_HEADS * IDX_DIM, IDX_HEADS, IDX_DIM)

NEG_BIG = -1e30
LOG2_E = 1.4426950408889634
INT_MIN = -2147483648
PAD_CHUNK = 1 << 30

NT_DIMS = (((1,), (1,)), ((), ()))


def _cparams(*sem):
    return pltpu.CompilerParams(dimension_semantics=sem, vmem_limit_bytes=VMEM_LIMIT)


def _round_up(n, m):
    return -(-n // m) * m


def _pick_tile(n, target):
    best = LANES
    for t in range(LANES, min(n, target) + 1, LANES):
        if n % t == 0:
            best = t
    return best


def _rot_tables(pos, rot_dim, offsets):
    half = rot_dim // 2
    inv = ROPE_THETA ** (-jnp.arange(half, dtype=F32) * (2.0 / rot_dim))
    freq_idx = np.zeros((LANES,), np.int32)
    first = np.zeros((LANES,), bool)
    second = np.zeros((LANES,), bool)
    for o in offsets:
        freq_idx[o:o + rot_dim] = np.arange(rot_dim) % half
        first[o:o + half] = True
        second[o + half:o + rot_dim] = True
    ang = pos.astype(F32)[:, None] * inv[freq_idx][None, :]
    cos, sin = jnp.cos(ang), jnp.sin(ang)
    c = jnp.where((first | second)[None, :], cos, 1.0)
    sm = jnp.where(first[None, :], -sin, 0.0)
    sp = jnp.where(second[None, :], sin, 0.0)
    return jnp.stack([c, sm, sp])


def _all_tables(pos):
    return jnp.concatenate([
        _rot_tables(pos, A_ROPE, (A_NOPE,)),
        _rot_tables(pos, B_ROT, (0, B_HEAD_DIM)),
        _rot_tables(pos, B_ROT, (0,)),
    ])


def _rot(x, tab_ref, base, half):
    c = tab_ref[base]
    sm = tab_ref[base + 1]
    sp = tab_ref[base + 2]
    return x * c + pltpu.roll(x, LANES - half, 1) * sm + pltpu.roll(x, half, 1) * sp


SLAB_CQ, SLAB_CKV, SLAB_KPE, SLAB_BQ, SLAB_BK, SLAB_BV = 0, 2, 3, 4, 8, 9
SLAB_IQ, SLAB_IW, SLAB_IK, SLAB_GA, N_SLAB_SMALL = 10, 14, 15, 16, 16


GATE_COLS = 512


def _cols(s0, n=1):
    return slice(s0 * LANES, (s0 + n) * LANES)


def _inproj_kernel(x_ref, ln1_ref, win_ref, wg_ref, gq_ref, gkv_ref, wuq_ref, gs_ref, tab_ref,
                   lat_ref, kpes_ref, bk_ref, bv_ref, ik_ref, qa_ref, qb_ref, iq_ref, iw_ref,
                   ga_ref, gb_ref):
    d = x_ref.shape[1]
    x = x_ref[...]
    h = x * lax.rsqrt(jnp.mean(x * x, axis=-1, keepdims=True) + EPS) * ln1_ref[...]
    hb = h.astype(BF16)
    lane = lax.broadcasted_iota(jnp.int32, (1, LANES), 1)
    lo = lane < B_HEAD_DIM

    def proj(s0, n=1):
        return jnp.dot(hb, win_ref[:, _cols(s0, n)], preferred_element_type=F32)

    def seg_rs(x2, mask, n):
        return lax.rsqrt(jnp.sum(jnp.where(mask, x2, 0.0), axis=-1, keepdims=True) * (1.0 / n) + EPS)

    cq = proj(SLAB_CQ, 2)
    cq = cq * lax.rsqrt(jnp.mean(cq * cq, axis=-1, keepdims=True) + EPS) * gq_ref[...]
    q = jnp.dot(cq.astype(BF16), wuq_ref[...], preferred_element_type=F32)
    rope_m = jnp.logical_and(lane >= A_NOPE, lane < A_NOPE + A_ROPE)
    for hd in range(A_HEADS):
        s = q[:, _cols(hd)]
        s2 = s * s
        sc = jnp.where(lo, seg_rs(s2, lo, A_NOPE), seg_rs(s2, rope_m, A_ROPE))
        qa_ref[:, _cols(hd)] = _rot(s * sc * gs_ref[0:1, :], tab_ref, 0, A_ROPE // 2).astype(BF16)

    both = proj(SLAB_CKV, 2)
    ckv = both[:, _cols(0)]
    lat_ref[...] = ckv * lax.rsqrt(jnp.mean(ckv * ckv, axis=-1, keepdims=True) + EPS) * gkv_ref[...]
    kp = both[:, _cols(1)]
    kp = kp * seg_rs(kp * kp, rope_m, A_ROPE) * gs_ref[1:2, :]
    kpes_ref[...] = _rot(kp, tab_ref, 0, A_ROPE // 2)

    bq = proj(SLAB_BQ, B_HEADS // 2)
    for j in range(B_HEADS // 2):
        s = bq[:, _cols(j)]
        s2 = s * s
        sc = jnp.where(lo, seg_rs(s2, lo, B_HEAD_DIM), seg_rs(s2, jnp.logical_not(lo), B_HEAD_DIM))
        s = _rot(s * sc * gs_ref[2:3, :], tab_ref, 3, B_ROT // 2) * B_SCALE
        qb_ref[:, _cols(j)] = s.astype(BF16)
    both = proj(SLAB_BK, 2)
    s = both[:, _cols(0)]
    s = s * seg_rs(s * s, lo, B_HEAD_DIM) * gs_ref[3:4, :]
    bk_ref[...] = _rot(s, tab_ref, 6, B_ROT // 2)[:, :B_HEAD_DIM]
    bv_ref[...] = both[:, LANES:LANES + B_HEAD_DIM]

    iq = proj(SLAB_IQ, IDX_HEADS // 2)
    for j in range(IDX_HEADS // 2):
        iq_ref[:, _cols(j)] = _rot(iq[:, _cols(j)], tab_ref, 3, IDX_ROT // 2).astype(BF16)
    both = proj(SLAB_IW, 2)
    iw_ref[...] = both[:, _cols(0)] * IDX_W_SCALE
    ik_ref[...] = _rot(both[:, _cols(1)], tab_ref, 6, IDX_ROT // 2)[:, :IDX_DIM]

    for j in range(0, d, GATE_COLS):
        ga_ref[:, j:j + GATE_COLS] = jax.nn.sigmoid(
            jnp.dot(hb, wg_ref[:, j:j + GATE_COLS], preferred_element_type=F32))
        gb_ref[:, j:j + GATE_COLS] = jax.nn.sigmoid(
            jnp.dot(hb, wg_ref[:, d + j:d + j + GATE_COLS], preferred_element_type=F32))


def _in_proj(x, tabs, reps, tm, w):
    n, d = x.shape
    r = tabs.shape[1]
    npb = r // tm
    assert r % tm == 0 and n == reps * r
    tok = lambda width: pl.BlockSpec((tm, width), lambda p, b: (b * npb + p, 0))
    full = lambda a: pl.BlockSpec(a.shape, lambda p, b: (0,) * a.ndim)
    outs = [
        ((n, A_KVRANK), F32), ((n, LANES), F32), ((n, B_HEAD_DIM), F32), ((n, B_HEAD_DIM), F32),
        ((n, IDX_DIM), F32), ((n, A_HEADS * LANES), BF16), ((n, B_HEADS * B_HEAD_DIM), BF16),
        ((n, IDX_HEADS * IDX_DIM), BF16), ((n, LANES), F32), ((n, d), F32), ((n, d), F32),
    ]
    return pl.pallas_call(
        _inproj_kernel,
        grid=(npb, reps),
        in_specs=[tok(d), full(w['ln1']), full(w['win']), full(w['wg']), full(w['gq']), full(w['gkv']),
                  full(w['wuq']), full(w['gs']),
                  pl.BlockSpec((tabs.shape[0], tm, LANES), lambda p, b: (0, p, 0))],
        out_specs=[tok(s[1]) for s, _ in outs],
        out_shape=[jax.ShapeDtypeStruct(s, t) for s, t in outs],
        compiler_params=_cparams("arbitrary", "arbitrary"),
        name="in_proj",
    )(x, w['ln1'], w['win'], w['wg'], w['gq'], w['gkv'], w['wuq'], w['gs'], tabs)


def _kvprep_kernel(lat_ref, kpes_ref, wk_ref, wv_ref, gk_ref, ka_ref, va_ref):
    lat = lat_ref[...].astype(BF16)
    kk = jnp.dot(lat, wk_ref[...], preferred_element_type=F32)
    vv = jnp.dot(lat, wv_ref[...], preferred_element_type=F32)
    kpes = kpes_ref[...]
    for hd in range(A_HEADS):
        s = kk[:, _cols(hd)]
        rs = lax.rsqrt(jnp.sum(s * s, axis=-1, keepdims=True) * (1.0 / A_NOPE) + EPS)
        ka_ref[hd] = (s * rs * gk_ref[...] + kpes).astype(BF16)
        va_ref[hd] = vv[:, _cols(hd)].astype(BF16)


def _kv_prep(lat_keys, kpes_keys, w):
    b, s, _ = lat_keys.shape
    ts = _pick_tile(s, 512)
    row = pl.BlockSpec((None, ts, LANES), lambda i, j: (i, j, 0))
    full = lambda a: pl.BlockSpec(a.shape, lambda i, j: (0,) * a.ndim)
    hd = pl.BlockSpec((None, A_HEADS, ts, LANES), lambda i, j: (i, 0, j, 0))
    shp = jax.ShapeDtypeStruct((b, A_HEADS, s, LANES), BF16)
    return pl.pallas_call(
        _kvprep_kernel,
        grid=(b, s // ts),
        in_specs=[row, row, full(w['wk']), full(w['wv']), full(w['gk'])],
        out_specs=[hd, hd],
        out_shape=[shp, shp],
        compiler_params=_cparams("arbitrary", "arbitrary"),
        name="kv_prep",
    )(lat_keys, kpes_keys, w['wk'], w['wv'], w['gk'])


def _mla_kernel(q_ref, k_ref, v_ref, qc_ref, kc_ref, o_ref, bias_s):
    j = pl.program_id(2)

    @pl.when(pl.program_id(1) == 0)
    def _():
        bias_s[j] = jnp.where(kc_ref[...] <= qc_ref[...], 0.0, NEG_BIG)

    s = lax.dot_general(q_ref[...], k_ref[...], NT_DIMS, preferred_element_type=F32) + bias_s[j]
    p = jnp.exp2((s - jnp.max(s, axis=-1, keepdims=True)) * (A_SCALE * LOG2_E))
    inv_l = 1.0 / jnp.sum(p, axis=-1, keepdims=True)
    o = jnp.dot(p.astype(BF16), v_ref[...], preferred_element_type=F32)
    o_ref[...] = (o * inv_l).astype(BF16)


def _mla_attn(qa, ka, va, qc, kc, tq, q0, nqg, s):
    n = qa.shape[0]
    b = ka.shape[0]
    nq = n // (b * tq)
    qspec = pl.BlockSpec((tq, LANES), lambda i, h, j: (i * nq + q0 + j, h))
    kspec = pl.BlockSpec((None, None, s, LANES), lambda i, h, j: (i, h, 0, 0))
    return pl.pallas_call(
        _mla_kernel,
        grid=(b, A_HEADS, nqg),
        in_specs=[qspec, kspec, kspec,
                  pl.BlockSpec((tq, 1), lambda i, h, j: (q0 + j, 0)),
                  pl.BlockSpec((1, s), lambda i, h, j: (0, 0))],
        out_specs=pl.BlockSpec((tq, LANES), lambda i, h, j: (i * nqg + j, h)),
        out_shape=jax.ShapeDtypeStruct((b * nqg * tq, A_HEADS * LANES), BF16),
        scratch_shapes=[pltpu.VMEM((nqg, tq, s), F32)],
        compiler_params=_cparams("arbitrary", "arbitrary", "arbitrary"),
        name="mla_attn",
    )(qa, ka, va, qc, kc)


def _dsa_kernel(iq_ref, iw_ref, qb_ref, ik2_ref, bk2_ref, bv2_ref, qc_ref, kc_ref, kidx_ref,
                o_ref, key_s, bias_s, *, k_sel, idx_bits):
    tq = key_s.shape[0]
    adm = kc_ref[...] <= qc_ref[...]
    iw = iw_ref[...]
    score = None
    for j in range(IDX_HEADS // 2):
        iqj = iq_ref[:, _cols(j)]
        for half in range(2):
            hd = 2 * j + half
            rel = jnp.maximum(
                lax.dot_general(iqj, ik2_ref[half], NT_DIMS, preferred_element_type=F32), 0.0)
            term = iw[:, hd:hd + 1] * rel
            score = term if score is None else score + term

    bits = lax.bitcast_convert_type(score, jnp.int32)
    key = jnp.where(bits < 0, bits ^ jnp.int32(0x7FFFFFFF), bits)
    key = jnp.where(key == -1, 0, key)
    key_s[...] = jnp.where(adm, key, INT_MIN)

    kf = float(k_sel)

    def count(mask):
        return jnp.sum(jnp.where(mask, 1.0, 0.0), axis=-1, keepdims=True)

    def value_step(i, cur):
        cand = cur + lax.shift_left(jnp.int32(1), 31 - i)
        return jnp.where(count(key_s[...] >= cand) >= kf, cand, cur)

    tau = lax.fori_loop(0, 32, value_step, jnp.full((tq, 1), INT_MIN, jnp.int32))
    key = key_s[...]
    ge = key >= tau
    c_ge = count(ge)
    bias_s[...] = jnp.where(jnp.logical_and(ge, adm), 0.0, NEG_BIG)
    amb = jnp.logical_and(tau > INT_MIN, c_ge > kf)

    @pl.when(jnp.max(jnp.where(amb, 1.0, 0.0)) > 0.0)
    def _():
        key = key_s[...]
        gt = key > tau
        tie = key == tau
        need = kf - count(gt)
        kidx = kidx_ref[...]

        def index_step(i, cur):
            cand = cur + lax.shift_left(jnp.int32(1), idx_bits - 1 - i)
            below = count(jnp.logical_and(tie, kidx < cand))
            return jnp.where(below < need, cand, cur)

        last = lax.fori_loop(0, idx_bits, index_step, jnp.zeros((tq, 1), jnp.int32))
        sel = jnp.logical_or(gt, jnp.logical_and(tie, kidx <= last))
        bias_s[...] = jnp.where(jnp.logical_and(sel, adm), 0.0, NEG_BIG)

    for j in range(B_HEADS // 2):
        qbj = qb_ref[:, _cols(j)]
        acc = None
        for half in range(2):
            lg = lax.dot_general(qbj, bk2_ref[half], NT_DIMS, preferred_element_type=F32)
            lg = lg + bias_s[...]
            p = jnp.exp(lg - jnp.max(lg, axis=-1, keepdims=True))
            inv_l = 1.0 / jnp.sum(p, axis=-1, keepdims=True)
            o = jnp.dot(p.astype(BF16), bv2_ref[half], preferred_element_type=F32) * inv_l
            acc = o if acc is None else acc + o
        o_ref[:, _cols(j)] = acc.astype(BF16)


def _dsa_attn(iq, iw, qb, ik2, bk2, bv2, qc, kc, kidx, tq, k_sel, q0, nqg, s):
    n = iq.shape[0]
    b, _, s_all, _ = ik2.shape
    nq = n // (b * tq)
    idx_bits = max(1, int(np.ceil(np.log2(s_all + N_META + 1))))
    tok = lambda width: pl.BlockSpec((tq, width), lambda i, j: (i * nq + q0 + j, 0))
    kspec = pl.BlockSpec((None, 2, s, LANES), lambda i, j: (i, 0, 0, 0))
    row = pl.BlockSpec((1, s), lambda i, j: (0, 0))
    return pl.pallas_call(
        functools.partial(_dsa_kernel, k_sel=k_sel, idx_bits=idx_bits),
        grid=(b, nqg),
        in_specs=[tok(iq.shape[1]), tok(LANES), tok(qb.shape[1]), kspec, kspec, kspec,
                  pl.BlockSpec((tq, 1), lambda i, j: (q0 + j, 0)), row, row],
        out_specs=pl.BlockSpec((tq, qb.shape[1]), lambda i, j: (i * nqg + j, 0)),
        out_shape=jax.ShapeDtypeStruct((b * nqg * tq, qb.shape[1]), BF16),
        scratch_shapes=[pltpu.VMEM((tq, s), jnp.int32), pltpu.VMEM((tq, s), F32)],
        compiler_params=_cparams("arbitrary", "arbitrary"),
        name="dsa_attn",
    )(iq, iw, qb, ik2, bk2, bv2, qc, kc, kidx)


def _outproj_kernel(x_ref, oa_ref, ob_ref, ga_ref, gb_ref, woa_ref, wob_ref, wout_ref, ln2_ref,
                    y_ref, t_ref):
    ya = jnp.dot(oa_ref[...], woa_ref[...], preferred_element_type=F32)
    yb = jnp.dot(ob_ref[...], wob_ref[...], preferred_element_type=F32)
    mg = ga_ref[...] * ya + gb_ref[...] * yb
    y = x_ref[...] + jnp.dot(mg.astype(BF16), wout_ref[...], preferred_element_type=F32)
    y_ref[...] = y
    t = y * lax.rsqrt(jnp.mean(y * y, axis=-1, keepdims=True) + EPS) * ln2_ref[...]
    t_ref[...] = t.astype(BF16)


def _out_proj(x, oa, ob, ga, gb, w, tm):
    n, d = x.shape
    tok = lambda a: pl.BlockSpec((tm, a.shape[1]), lambda i: (i, 0))
    full = lambda a: pl.BlockSpec(a.shape, lambda i: (0,) * a.ndim)
    return pl.pallas_call(
        _outproj_kernel,
        grid=(n // tm,),
        in_specs=[tok(x), tok(oa), tok(ob), tok(ga), tok(gb),
                  full(w['woa']), full(w['wob']), full(w['wout']), full(w['ln2'])],
        out_specs=[tok(x), tok(x)],
        out_shape=[jax.ShapeDtypeStruct((n, d), F32), jax.ShapeDtypeStruct((n, d), BF16)],
        compiler_params=_cparams("arbitrary"),
        name="out_proj",
    )(x, oa, ob, ga, gb, w['woa'], w['wob'], w['wout'], w['ln2'])


def _top_rows(s, k, with_rank=False):
    t = s.shape[1]
    row = lax.broadcasted_iota(jnp.int32, (k, t), 0).astype(F32)
    out = jnp.zeros((k, t), F32)
    seen = jnp.zeros((1, t), F32)
    rank = jnp.full(s.shape, float(k), F32)
    for _ in range(k):
        m = jnp.max(s, axis=0, keepdims=True)
        eq = s == m
        if with_rank:
            rank = jnp.where(eq, seen, rank)
        upto = seen + jnp.sum(jnp.where(eq, 1.0, 0.0), axis=0, keepdims=True)
        out = jnp.where(jnp.logical_and(row >= seen, row < upto), m, out)
        seen = upto
        s = jnp.where(eq, -jnp.inf, s)
    return (out, rank) if with_rank else out


def _top_rows_distinct(s, k, with_rank):
    t = s.shape[1]
    row = lax.broadcasted_iota(jnp.int32, (k, t), 0)
    out = jnp.zeros((k, t), F32)
    rank = jnp.full(s.shape, float(k), F32)
    for q in range(k):
        m = jnp.max(s, axis=0, keepdims=True)
        eq = s == m
        if with_rank:
            rank = jnp.where(eq, float(q), rank)
        out = jnp.where(row == q, m, out)
        s = jnp.where(eq, -jnp.inf, s)
    used = jnp.sum(jnp.where(s == -jnp.inf, 1.0, 0.0), axis=0, keepdims=True)
    return out, rank, used


def _top_rows_into(s, k, n_inf, out_s, rank_s=None):
    out, rank, used = _top_rows_distinct(s, k, rank_s is not None)
    out_s[...] = out
    if rank_s is not None:
        rank_s[...] = rank

    @pl.when(jnp.max(used) > float(k + n_inf))
    def _():
        if rank_s is None:
            out_s[...] = _top_rows(s, k)
        else:
            out_s[...], rank_s[...] = _top_rows(s, k, with_rank=True)


def _pair_sum_candidates(a, b):
    k = PEER_TOPK
    assert k == 16 and a.shape[0] == k
    row = lax.broadcasted_iota(jnp.int32, (8, a.shape[1]), 0)
    blocks = [a[0:1] + b]
    n_inf = 0
    for i in range(1, 8):
        blocks.append(jnp.where(row < k // (i + 1), a[i:i + 1] + b[:8], -jnp.inf))
        n_inf += 8 - k // (i + 1)
    blocks.append(a[8:] + b[0:1])
    return jnp.concatenate(blocks, axis=0), n_inf


def _peer_kernel(t_ref, y_ref, wqt_ref, sk1_ref, sk2_ref, u_ref, vt_ref, o_ref,
                 m_s, r_s, a_s, b_s, gw_s, acc_s, qt_s, t1_s, t2_s, tp_s, rk_s):
    c = pl.program_id(1)
    nc = pl.num_programs(1)
    te = u_ref.shape[0]
    half = PEER_DKEY // 2

    @pl.when(c == 0)
    def _route():
        qt = lax.dot_general(wqt_ref[...], t_ref[...], NT_DIMS, preferred_element_type=F32)
        qt_s[...] = qt.astype(BF16)

        @pl.loop(0, PEER_HEADS)
        def _head(hd):
            row0 = pl.multiple_of(hd * PEER_DKEY, PEER_DKEY)
            q1 = qt_s[pl.ds(row0, half), :]
            q2 = qt_s[pl.ds(row0 + half, half), :]
            s1 = jnp.dot(sk1_ref[hd], q1, preferred_element_type=F32)
            s2 = jnp.dot(sk2_ref[hd], q2, preferred_element_type=F32)
            _top_rows_into(s1, PEER_TOPK, 0, t1_s)
            _top_rows_into(s2, PEER_TOPK, 0, t2_s, rk_s)
            top1, top2, r = t1_s[...], t2_s[...], rk_s[...]
            cand, n_inf = _pair_sum_candidates(top1, top2)
            _top_rows_into(cand, PEER_TOPK, n_inf, tp_s)
            top = tp_s[...]
            tau = top[PEER_TOPK - 1:PEER_TOPK]
            z = jnp.sum(jnp.exp(top - top[0:1]), axis=0, keepdims=True)
            m = jnp.zeros(s1.shape, F32)
            for j in range(PEER_TOPK):
                reach = top1 + top2[j:j + 1] >= tau
                c_j = jnp.min(jnp.where(reach, top1, jnp.inf), axis=0, keepdims=True)
                m = m + jnp.where(s1 >= c_j, 1.0, 0.0)
            m_s[hd] = m
            r_s[hd] = r.astype(BF16)
            a_s[hd] = jnp.exp(s1 - top1[0:1])
            b_s[hd] = (jnp.exp(s2 - top2[0:1]) * (1.0 / z)).astype(BF16)
        acc_s[...] = jnp.zeros_like(acc_s)

    at = lax.dot_general(u_ref[...], t_ref[...], NT_DIMS, preferred_element_type=F32)
    zero = jnp.zeros((), BF16)
    tm = t_ref.shape[0]
    pk = (N_KEYS // BF16_ROWS, BF16_ROWS, tm)
    for k in range(te // N_KEYS):
        i1 = c * (te // N_KEYS) + k
        w = None
        for hd in range(PEER_HEADS):
            m_row = jnp.broadcast_to(m_s[hd, pl.ds(i1, 1), :], pk[1:]).astype(BF16)
            a_row = jnp.broadcast_to(a_s[hd, pl.ds(i1, 1), :], pk[1:]).astype(BF16)
            sel = r_s[hd].reshape(pk) < m_row[None]
            term = jnp.where(sel, b_s[hd].reshape(pk) * a_row[None], zero)
            w = term if w is None else w + term
        g = jax.nn.gelu(at[k * N_KEYS:(k + 1) * N_KEYS])
        gw_s[k * N_KEYS:(k + 1) * N_KEYS, :] = w.reshape(N_KEYS, tm) * g.astype(BF16)
    acc_s[...] += jnp.dot(vt_ref[...], gw_s[...], preferred_element_type=F32)

    @pl.when(c == nc - 1)
    def _finish():
        o_ref[...] = y_ref[...] + acc_s[...].T


def _peer(t, y, w, tm, te):
    n, d = y.shape
    nc = w['u'].shape[0] // te
    full = lambda a: pl.BlockSpec(a.shape, lambda i, c: (0,) * a.ndim)
    tok = pl.BlockSpec((tm, d), lambda i, c: (i, 0))
    hk = (PEER_HEADS, N_KEYS, tm)
    return pl.pallas_call(
        _peer_kernel,
        grid=(n // tm, nc),
        in_specs=[tok, tok, full(w['wqt']), full(w['sk1']), full(w['sk2']),
                  pl.BlockSpec((te, d), lambda i, c: (c, 0)),
                  pl.BlockSpec((d, te), lambda i, c: (0, c))],
        out_specs=tok,
        out_shape=jax.ShapeDtypeStruct((n, d), F32),
        scratch_shapes=[pltpu.VMEM(hk, F32), pltpu.VMEM(hk, BF16), pltpu.VMEM(hk, F32),
                        pltpu.VMEM(hk, BF16), pltpu.VMEM((te, tm), BF16), pltpu.VMEM((d, tm), F32),
                        pltpu.VMEM((PEER_HEADS * PEER_DKEY, tm), BF16),
                        pltpu.VMEM((PEER_TOPK, tm), F32), pltpu.VMEM((PEER_TOPK, tm), F32),
                        pltpu.VMEM((PEER_TOPK, tm), F32), pltpu.VMEM((N_KEYS, tm), F32)],
        compiler_params=_cparams("arbitrary", "arbitrary"),
        name="peer",
    )(t, y, w['wqt'], w['sk1'], w['sk2'], w['u'], w['vt'])


def _prep_weights(ln1_g, w_in, a_q_norm_g, a_kv_norm_g, a_w_uq, a_w_ukv, a_qk_g, b_qk_g,
                  w_o_a, w_o_b, w_out, ln2_g, peer_wq, peer_subkeys, peer_u, peer_v):
    d = w_in.shape[0]
    offs = np.cumsum(IN_SPLITS).tolist()
    cq, ckv, kpe, bq, bk, bv, iq, iw, ik, gates = jnp.split(w_in, offs, axis=1)
    z = lambda n: jnp.zeros((d, n), w_in.dtype)
    win = jnp.concatenate([
        cq, ckv, z(A_NOPE), kpe, z(LANES - A_NOPE - A_ROPE), bq,
        bk, z(LANES - B_HEAD_DIM), bv, z(LANES - B_HEAD_DIM), iq,
        iw, z(LANES - IDX_HEADS), ik, z(LANES - IDX_DIM)], axis=1).astype(BF16)
    assert win.shape[1] == N_SLAB_SMALL * LANES

    def pad_last(a, n):
        return jnp.pad(a, [(0, 0)] * (a.ndim - 1) + [(0, n - a.shape[-1])])

    wuq = pad_last(a_w_uq.reshape(A_QRANK, A_HEADS, A_NOPE + A_ROPE), LANES)
    wuq = wuq.reshape(A_QRANK, A_HEADS * LANES).astype(BF16)
    ukv = a_w_ukv.reshape(A_KVRANK, A_HEADS, A_NOPE + A_V)
    wk = pad_last(ukv[..., :A_NOPE], LANES).reshape(A_KVRANK, A_HEADS * LANES).astype(BF16)
    wv = pad_last(ukv[..., A_NOPE:], LANES).reshape(A_KVRANK, A_HEADS * LANES).astype(BF16)
    woa = pad_last(w_o_a.reshape(A_HEADS, A_V, d).transpose(0, 2, 1), LANES)
    woa = woa.transpose(0, 2, 1).reshape(A_HEADS * LANES, d).astype(BF16)

    row = lambda v: pad_last(v, LANES)[None, :]
    gs = jnp.concatenate([
        row(a_qk_g[0]),
        row(jnp.concatenate([jnp.zeros((A_NOPE,), F32), a_qk_g[1, A_NOPE:]])),
        row(jnp.concatenate([b_qk_g[0], b_qk_g[0]])),
        row(b_qk_g[1]),
        jnp.zeros((4, LANES), F32)], axis=0)
    return dict(
        ln1=ln1_g[None, :], win=win, wg=gates.astype(BF16), gq=a_q_norm_g[None, :],
        gkv=a_kv_norm_g[None, :], wuq=wuq, gs=gs, wk=wk, wv=wv, gk=row(a_qk_g[1, :A_NOPE]),
        woa=woa, wob=w_o_b.astype(BF16), wout=w_out.astype(BF16), ln2=ln2_g[None, :],
        wqt=peer_wq.T.astype(BF16),
        sk1=peer_subkeys[:, 0].astype(BF16), sk2=peer_subkeys[:, 1].astype(BF16),
        u=peer_u.astype(BF16), vt=peer_v.T.astype(BF16))


def _key_layout(meta, rows):
    pad = jnp.zeros((meta.shape[0], LANES - meta.shape[1], meta.shape[2]), meta.dtype)
    return jnp.concatenate([meta, pad, rows], axis=1)


def _pair_slabs(k):
    kb = k.astype(BF16)
    z = jnp.zeros_like(kb)
    return jnp.stack([jnp.concatenate([kb, z], -1), jnp.concatenate([z, kb], -1)], axis=1)


def _mixers_and_peer(x, rows, keys, qc, kc, kidx, k_sel, w, tq, tm, te, groups):
    lat_k, kpes_k, bk_k, bv_k, ik_k = keys
    (_, _, _, _, _, qa, qb, iq, iw, ga, gb) = rows
    b = lat_k.shape[0]
    ka, va = _kv_prep(lat_k, kpes_k, w)
    ik2, bk2, bv2 = _pair_slabs(ik_k), _pair_slabs(bk_k), _pair_slabs(bv_k)
    oa, ob = [], []
    for q0, nqg, s in groups:
        o = _mla_attn(qa, ka, va, qc, kc, tq, q0, nqg, s)
        oa.append(o.reshape(b, nqg * tq, o.shape[-1]))
        o = _dsa_attn(iq, iw, qb, ik2, bk2, bv2, qc, kc, kidx, tq, k_sel, q0, nqg, s)
        ob.append(o.reshape(b, nqg * tq, o.shape[-1]))
    oa = jnp.concatenate(oa, axis=1).reshape(x.shape[0], -1)
    ob = jnp.concatenate(ob, axis=1).reshape(x.shape[0], -1)
    y1, t2 = _out_proj(x, oa, ob, ga, gb, w, tm)
    tp = 2 * tm if x.shape[0] % (2 * tm) == 0 else tm
    return _peer(t2, y1, w, tp, te)


def kernel(x_prompt, x_sample, cache_a_latent, cache_a_kpe, cache_b_k, cache_b_v, cache_b_idx_k,
           meta_tokens, ln1_g, w_in, a_q_norm_g, a_kv_norm_g, a_w_uq, a_w_ukv, a_qk_g, b_qk_g,
           w_o_a, w_o_b, w_out, ln2_g, peer_wq, peer_subkeys, peer_u, peer_v):
    bp, sp, d = x_prompt.shape
    bs, ts, _ = x_sample.shape
    depth, _, past, _ = cache_a_latent.shape
    assert depth == 1 and sp % CHUNK == 0
    w = _prep_weights(ln1_g[0], w_in[0], a_q_norm_g[0], a_kv_norm_g[0], a_w_uq[0], a_w_ukv[0],
                      a_qk_g[0], b_qk_g[0], w_o_a[0], w_o_b[0], w_out[0], ln2_g[0], peer_wq[0],
                      peer_subkeys[0], peer_u[0], peer_v[0])
    tm = 256
    te = 512
    i32 = jnp.int32

    xq = x_prompt.reshape(bp * sp, d)
    rows_p = _in_proj(xq, _all_tables(N_META + jnp.arange(sp, dtype=i32)), bp, tm, w)
    rows_m = _in_proj(meta_tokens.astype(F32), _all_tables(jnp.arange(N_META, dtype=i32)), 1, N_META, w)
    xs = x_sample.reshape(bs * ts, d)
    pos_s = past + jnp.arange(ts, dtype=i32)
    rows_s = _in_proj(xs, _all_tables(jnp.tile(pos_s, bs)), 1, bs * ts, w)

    s_p = LANES + sp
    per_b = lambda a, b, t: a.reshape(b, t, a.shape[-1])
    meta_b = lambda a: jnp.broadcast_to(a[None], (bp,) + a.shape)
    keys_p = tuple(_key_layout(meta_b(m), per_b(r, bp, sp)) for r, m in zip(rows_p[:5], rows_m[:5]))
    chunk_q = jnp.arange(sp, dtype=i32) // CHUNK
    n_pad = LANES - N_META
    kc_p = jnp.concatenate([jnp.full((N_META,), -1, i32), jnp.full((n_pad,), PAD_CHUNK, i32),
                            chunk_q])[None, :]
    kidx_p = jnp.concatenate([jnp.arange(N_META, dtype=i32), s_p + jnp.arange(n_pad, dtype=i32),
                              N_META + jnp.arange(sp, dtype=i32)])[None, :]
    tq = 256
    nq = sp // tq
    tpg = max(1, nq // PROMPT_GROUPS)
    groups_p = [(q0, min(tpg, nq - q0), LANES + min(q0 + tpg, nq) * tq) for q0 in range(0, nq, tpg)]
    y_p = _mixers_and_peer(xq, rows_p, keys_p, chunk_q[:, None], kc_p, kidx_p,
                           min(IDX_TOPK, sp // 4), w, tq, tm, te, groups_p)

    s_s = _round_up(past + ts, LANES)
    kpes_cache = jnp.pad(cache_a_kpe[0], ((0, 0), (0, 0), (A_NOPE, LANES - A_NOPE - A_ROPE)))
    caches = (cache_a_latent[0], kpes_cache, cache_b_k[0], cache_b_v[0], cache_b_idx_k[0])
    keys_s = tuple(jnp.pad(jnp.concatenate([c.astype(F32), per_b(r, bs, ts)], axis=1),
                           ((0, 0), (0, s_s - past - ts), (0, 0)))
                   for c, r in zip(caches, rows_s[:5]))
    kc_s = jnp.concatenate([jnp.zeros((past,), i32), jnp.ones((ts,), i32),
                            jnp.full((s_s - past - ts,), PAD_CHUNK, i32)])[None, :]
    kidx_s = jnp.arange(s_s, dtype=i32)[None, :]
    y_s = _mixers_and_peer(xs, rows_s, keys_s, jnp.ones((ts, 1), i32), kc_s, kidx_s,
                           min(IDX_TOPK, (past + ts) // 4), w, ts, bs * ts, te, [(0, 1, s_s)])

    kpe_of = lambda r: r[:, A_NOPE:A_NOPE + A_ROPE]

    def new_rows_p(r, m):
        return jnp.concatenate([meta_b(m), per_b(r, bp, sp)], axis=1)[None]

    outs_p = [new_rows_p(rows_p[0], rows_m[0]), new_rows_p(kpe_of(rows_p[1]), kpe_of(rows_m[1]))]
    outs_p += [new_rows_p(rows_p[i], rows_m[i]) for i in (2, 3, 4)]
    outs_s = [per_b(rows_s[0], bs, ts)[None], per_b(kpe_of(rows_s[1]), bs, ts)[None]]
    outs_s += [per_b(rows_s[i], bs, ts)[None] for i in (2, 3, 4)]
    return (y_p.reshape(bp, sp, d), y_s.reshape(bs, ts, d), *outs_p, *outs_s)
```

```python
import functools

import jax
import jax.numpy as jnp
import numpy as np
from jax import lax
from jax.experimental import pallas as pl
from jax.experimental.pallas import tpu as pltpu

F32 = jnp.float32
BF16 = jnp.bfloat16

LANES = 128
BF16_ROWS = 16
VMEM_LIMIT = 52 * 1024 * 1024

CHUNK = 64
N_META = 16
ROPE_THETA = 500000.0
EPS = 1e-6

A_HEADS = 8
A_NOPE = 64
A_ROPE = 32
A_V = 64
A_QRANK = 256
A_KVRANK = 128
A_SCALE = (A_NOPE + A_ROPE) ** -0.5

B_HEADS = 8
B_HEAD_DIM = 64
B_ROT = B_HEAD_DIM // 4
B_SCALE = B_HEAD_DIM ** -0.5
IDX_HEADS = 8
IDX_DIM = 64
IDX_ROT = IDX_DIM // 4
IDX_TOPK = 256
IDX_W_SCALE = (IDX_HEADS * IDX_DIM) ** -0.5

PEER_HEADS = 8
PEER_DKEY = 128
N_KEYS = 128
PEER_TOPK = 16
PEER_LINK_EVERY = 2
PROMPT_GROUPS = 16

IN_SPLITS = (A_QRANK, A_KVRANK, A_ROPE, B_HEADS * B_HEAD_DIM, B_HEAD_DIM, B_HEAD_DIM,
             IDX_HEADS * IDX_DIM, IDX_HEADS, IDX_DIM)

NEG_BIG = -1e30
LOG2_E = 1.4426950408889634
INT_MIN = -2147483648
PAD_CHUNK = 1 << 30

NT_DIMS = (((1,), (1,)), ((), ()))


def _cparams(*sem):
    return pltpu.CompilerParams(dimension_semantics=sem, vmem_limit_bytes=VMEM_LIMIT)


def _round_up(n, m):
    return -(-n // m) * m


def _pick_tile(n, target):
    best = LANES
    for t in range(LANES, min(n, target) + 1, LANES):
        if n % t == 0:
            best = t
    return best


def _rot_tables(pos, rot_dim, offsets):
    half = rot_dim // 2
    inv = ROPE_THETA ** (-jnp.arange(half, dtype=F32) * (2.0 / rot_dim))
    freq_idx = np.zeros((LANES,), np.int32)
    first = np.zeros((LANES,), bool)
    second = np.zeros((LANES,), bool)
    for o in offsets:
        freq_idx[o:o + rot_dim] = np.arange(rot_dim) % half
        first[o:o + half] = True
        second[o + half:o + rot_dim] = True
    ang = pos.astype(F32)[:, None] * inv[freq_idx][None, :]
    cos, sin = jnp.cos(ang), jnp.sin(ang)
    c = jnp.where((first | second)[None, :], cos, 1.0)
    sm = jnp.where(first[None, :], -sin, 0.0)
    sp = jnp.where(second[None, :], sin, 0.0)
    return jnp.stack([c, sm, sp])


def _all_tables(pos):
    return jnp.concatenate([
        _rot_tables(pos, A_ROPE, (A_NOPE,)),
        _rot_tables(pos, B_ROT, (0, B_HEAD_DIM)),
        _rot_tables(pos, B_ROT, (0,)),
    ])


def _rot(x, tab_ref, base, half):
    c = tab_ref[base]
    sm = tab_ref[base + 1]
    sp = tab_ref[base + 2]
    return x * c + pltpu.roll(x, LANES - half, 1) * sm + pltpu.roll(x, half, 1) * sp


SLAB_CQ, SLAB_CKV, SLAB_KPE, SLAB_BQ, SLAB_BK, SLAB_BV = 0, 2, 3, 4, 8, 9
SLAB_IQ, SLAB_IW, SLAB_IK, SLAB_GA, N_SLAB_SMALL = 10, 14, 15, 16, 16


GATE_COLS = 512


def _cols(s0, n=1):
    return slice(s0 * LANES, (s0 + n) * LANES)


def _inproj_kernel(x_ref, ln1_ref, win_ref, wg_ref, gq_ref, gkv_ref, wuq_ref, gs_ref, tab_ref,
                   lat_ref, kpes_ref, bk_ref, bv_ref, ik_ref, qa_ref, qb_ref, iq_ref, iw_ref,
                   ga_ref, gb_ref):
    d = x_ref.shape[1]
    x = x_ref[...]
    h = x * lax.rsqrt(jnp.mean(x * x, axis=-1, keepdims=True) + EPS) * ln1_ref[...]
    hb = h.astype(BF16)
    lane = lax.broadcasted_iota(jnp.int32, (1, LANES), 1)
    lo = lane < B_HEAD_DIM

    def proj(s0, n=1):
        return jnp.dot(hb, win_ref[:, _cols(s0, n)], preferred_element_type=F32)

    def seg_rs(x2, mask, n):
        return lax.rsqrt(jnp.sum(jnp.where(mask, x2, 0.0), axis=-1, keepdims=True) * (1.0 / n) + EPS)

    cq = proj(SLAB_CQ, 2)
    cq = cq * lax.rsqrt(jnp.mean(cq * cq, axis=-1, keepdims=True) + EPS) * gq_ref[...]
    q = jnp.dot(cq.astype(BF16), wuq_ref[...], preferred_element_type=F32)
    rope_m = jnp.logical_and(lane >= A_NOPE, lane < A_NOPE + A_ROPE)
    for hd in range(A_HEADS):
        s = q[:, _cols(hd)]
        s2 = s * s
        sc = jnp.where(lo, seg_rs(s2, lo, A_NOPE), seg_rs(s2, rope_m, A_ROPE))
        qa_ref[:, _cols(hd)] = _rot(s * sc * gs_ref[0:1, :], tab_ref, 0, A_ROPE // 2).astype(BF16)

    both = proj(SLAB_CKV, 2)
    ckv = both[:, _cols(0)]
    lat_ref[...] = ckv * lax.rsqrt(jnp.mean(ckv * ckv, axis=-1, keepdims=True) + EPS) * gkv_ref[...]
    kp = both[:, _cols(1)]
    kp = kp * seg_rs(kp * kp, rope_m, A_ROPE) * gs_ref[1:2, :]
    kpes_ref[...] = _rot(kp, tab_ref, 0, A_ROPE // 2)

    bq = proj(SLAB_BQ, B_HEADS // 2)
    for j in range(B_HEADS // 2):
        s = bq[:, _cols(j)]
        s2 = s * s
        sc = jnp.where(lo, seg_rs(s2, lo, B_HEAD_DIM), seg_rs(s2, jnp.logical_not(lo), B_HEAD_DIM))
        s = _rot(s * sc * gs_ref[2:3, :], tab_ref, 3, B_ROT // 2) * B_SCALE
        qb_ref[:, _cols(j)] = s.astype(BF16)
    both = proj(SLAB_BK, 2)
    s = both[:, _cols(0)]
    s = s * seg_rs(s * s, lo, B_HEAD_DIM) * gs_ref[3:4, :]
    bk_ref[...] = _rot(s, tab_ref, 6, B_ROT // 2)[:, :B_HEAD_DIM]
    bv_ref[...] = both[:, LANES:LANES + B_HEAD_DIM]

    iq = proj(SLAB_IQ, IDX_HEADS // 2)
    for j in range(IDX_HEADS // 2):
        iq_ref[:, _cols(j)] = _rot(iq[:, _cols(j)], tab_ref, 3, IDX_ROT // 2).astype(BF16)
    both = proj(SLAB_IW, 2)
    iw_ref[...] = both[:, _cols(0)] * IDX_W_SCALE
    ik_ref[...] = _rot(both[:, _cols(1)], tab_ref, 6, IDX_ROT // 2)[:, :IDX_DIM]

    for j in range(0, d, GATE_COLS):
        ga_ref[:, j:j + GATE_COLS] = jax.nn.sigmoid(
            jnp.dot(hb, wg_ref[:, j:j + GATE_COLS], preferred_element_type=F32))
        gb_ref[:, j:j + GATE_COLS] = jax.nn.sigmoid(
            jnp.dot(hb, wg_ref[:, d + j:d + j + GATE_COLS], preferred_element_type=F32))


def _in_proj(x, tabs, reps, tm, w):
    n, d = x.shape
    r = tabs.shape[1]
    npb = r // tm
    assert r % tm == 0 and n == reps * r
    tok = lambda width: pl.BlockSpec((tm, width), lambda p, b: (b * npb + p, 0))
    full = lambda a: pl.BlockSpec(a.shape, lambda p, b: (0,) * a.ndim)
    outs = [
        ((n, A_KVRANK), F32), ((n, LANES), F32), ((n, B_HEAD_DIM), F32), ((n, B_HEAD_DIM), F32),
        ((n, IDX_DIM), F32), ((n, A_HEADS * LANES), BF16), ((n, B_HEADS * B_HEAD_DIM), BF16),
        ((n, IDX_HEADS * IDX_DIM), BF16), ((n, LANES), F32), ((n, d), F32), ((n, d), F32),
    ]
    return pl.pallas_call(
        _inproj_kernel,
        grid=(npb, reps),
        in_specs=[tok(d), full(w['ln1']), full(w['win']), full(w['wg']), full(w['gq']), full(w['gkv']),
                  full(w['wuq']), full(w['gs']),
                  pl.BlockSpec((tabs.shape[0], tm, LANES), lambda p, b: (0, p, 0))],
        out_specs=[tok(s[1]) for s, _ in outs],
        out_shape=[jax.ShapeDtypeStruct(s, t) for s, t in outs],
        compiler_params=_cparams("arbitrary", "arbitrary"),
        name="in_proj",
    )(x, w['ln1'], w['win'], w['wg'], w['gq'], w['gkv'], w['wuq'], w['gs'], tabs)


def _kvprep_kernel(lat_ref, kpes_ref, wk_ref, wv_ref, gk_ref, ka_ref, va_ref):
    lat = lat_ref[...].astype(BF16)
    kk = jnp.dot(lat, wk_ref[...], preferred_element_type=F32)
    vv = jnp.dot(lat, wv_ref[...], preferred_element_type=F32)
    kpes = kpes_ref[...]
    for hd in range(A_HEADS):
        s = kk[:, _cols(hd)]
        rs = lax.rsqrt(jnp.sum(s * s, axis=-1, keepdims=True) * (1.0 / A_NOPE) + EPS)
        ka_ref[hd] = (s * rs * gk_ref[...] + kpes).astype(BF16)
        va_ref[hd] = vv[:, _cols(hd)].astype(BF16)


def _kv_prep(lat_keys, kpes_keys, w):
    b, s, _ = lat_keys.shape
    ts = _pick_tile(s, 512)
    row = pl.BlockSpec((None, ts, LANES), lambda i, j: (i, j, 0))
    full = lambda a: pl.BlockSpec(a.shape, lambda i, j: (0,) * a.ndim)
    hd = pl.BlockSpec((None, A_HEADS, ts, LANES), lambda i, j: (i, 0, j, 0))
    shp = jax.ShapeDtypeStruct((b, A_HEADS, s, LANES), BF16)
    return pl.pallas_call(
        _kvprep_kernel,
        grid=(b, s // ts),
        in_specs=[row, row, full(w['wk']), full(w['wv']), full(w['gk'])],
        out_specs=[hd, hd],
        out_shape=[shp, shp],
        compiler_params=_cparams("arbitrary", "arbitrary"),
        name="kv_prep",
    )(lat_keys, kpes_keys, w['wk'], w['wv'], w['gk'])


def _mla_kernel(q_ref, k_ref, v_ref, qc_ref, kc_ref, o_ref, bias_s):
    j = pl.program_id(2)

    @pl.when(pl.program_id(1) == 0)
    def _():
        bias_s[j] = jnp.where(kc_ref[...] <= qc_ref[...], 0.0, NEG_BIG)

    s = lax.dot_general(q_ref[...], k_ref[...], NT_DIMS, preferred_element_type=F32) + bias_s[j]
    p = jnp.exp2((s - jnp.max(s, axis=-1, keepdims=True)) * (A_SCALE * LOG2_E))
    inv_l = 1.0 / jnp.sum(p, axis=-1, keepdims=True)
    o = jnp.dot(p.astype(BF16), v_ref[...], preferred_element_type=F32)
    o_ref[...] = (o * inv_l).astype(BF16)


def _mla_attn(qa, ka, va, qc, kc, tq, q0, nqg, s):
    n = qa.shape[0]
    b = ka.shape[0]
    nq = n // (b * tq)
    qspec = pl.BlockSpec((tq, LANES), lambda i, h, j: (i * nq + q0 + j, h))
    kspec = pl.BlockSpec((None, None, s, LANES), lambda i, h, j: (i, h, 0, 0))
    return pl.pallas_call(
        _mla_kernel,
        grid=(b, A_HEADS, nqg),
        in_specs=[qspec, kspec, kspec,
                  pl.BlockSpec((tq, 1), lambda i, h, j: (q0 + j, 0)),
                  pl.BlockSpec((1, s), lambda i, h, j: (0, 0))],
        out_specs=pl.BlockSpec((tq, LANES), lambda i, h, j: (i * nqg + j, h)),
        out_shape=jax.ShapeDtypeStruct((b * nqg * tq, A_HEADS * LANES), BF16),
        scratch_shapes=[pltpu.VMEM((nqg, tq, s), F32)],
        compiler_params=_cparams("arbitrary", "arbitrary", "arbitrary"),
        name="mla_attn",
    )(qa, ka, va, qc, kc)


def _dsa_kernel(iq_ref, iw_ref, qb_ref, ik2_ref, bk2_ref, bv2_ref, qc_ref, kc_ref, kidx_ref,
                o_ref, key_s, bias_s, *, k_sel, idx_bits):
    tq = key_s.shape[0]
    adm = kc_ref[...] <= qc_ref[...]
    iw = iw_ref[...]
    score = None
    for j in range(IDX_HEADS // 2):
        iqj = iq_ref[:, _cols(j)]
        for half in range(2):
            hd = 2 * j + half
            rel = jnp.maximum(
                lax.dot_general(iqj, ik2_ref[half], NT_DIMS, preferred_element_type=F32), 0.0)
            term = iw[:, hd:hd + 1] * rel
            score = term if score is None else score + term

    bits = lax.bitcast_convert_type(score, jnp.int32)
    key = jnp.where(bits < 0, bits ^ jnp.int32(0x7FFFFFFF), bits)
    key = jnp.where(key == -1, 0, key)
    key_s[...] = jnp.where(adm, key, INT_MIN)

    kf = float(k_sel)

    def count(mask):
        return jnp.sum(jnp.where(mask, 1.0, 0.0), axis=-1, keepdims=True)

    def value_step(i, cur):
        cand = cur + lax.shift_left(jnp.int32(1), 31 - i)
        return jnp.where(count(key_s[...] >= cand) >= kf, cand, cur)

    tau = lax.fori_loop(0, 32, value_step, jnp.full((tq, 1), INT_MIN, jnp.int32))
    key = key_s[...]
    ge = key >= tau
    c_ge = count(ge)
    bias_s[...] = jnp.where(jnp.logical_and(ge, adm), 0.0, NEG_BIG)
    amb = jnp.logical_and(tau > INT_MIN, c_ge > kf)

    @pl.when(jnp.max(jnp.where(amb, 1.0, 0.0)) > 0.0)
    def _():
        key = key_s[...]
        gt = key > tau
        tie = key == tau
        need = kf - count(gt)
        kidx = kidx_ref[...]

        def index_step(i, cur):
            cand = cur + lax.shift_left(jnp.int32(1), idx_bits - 1 - i)
            below = count(jnp.logical_and(tie, kidx < cand))
            return jnp.where(below < need, cand, cur)

        last = lax.fori_loop(0, idx_bits, index_step, jnp.zeros((tq, 1), jnp.int32))
        sel = jnp.logical_or(gt, jnp.logical_and(tie, kidx <= last))
        bias_s[...] = jnp.where(jnp.logical_and(sel, adm), 0.0, NEG_BIG)

    for j in range(B_HEADS // 2):
        qbj = qb_ref[:, _cols(j)]
        acc = None
        for half in range(2):
            lg = lax.dot_general(qbj, bk2_ref[half], NT_DIMS, preferred_element_type=F32)
            lg = lg + bias_s[...]
            p = jnp.exp(lg - jnp.max(lg, axis=-1, keepdims=True))
            inv_l = 1.0 / jnp.sum(p, axis=-1, keepdims=True)
            o = jnp.dot(p.astype(BF16), bv2_ref[half], preferred_element_type=F32) * inv_l
            acc = o if acc is None else acc + o
        o_ref[:, _cols(j)] = acc.astype(BF16)


def _dsa_attn(iq, iw, qb, ik2, bk2, bv2, qc, kc, kidx, tq, k_sel, q0, nqg, s):
    n = iq.shape[0]
    b, _, s_all, _ = ik2.shape
    nq = n // (b * tq)
    idx_bits = max(1, int(np.ceil(np.log2(s_all + N_META + 1))))
    tok = lambda width: pl.BlockSpec((tq, width), lambda i, j: (i * nq + q0 + j, 0))
    kspec = pl.BlockSpec((None, 2, s, LANES), lambda i, j: (i, 0, 0, 0))
    row = pl.BlockSpec((1, s), lambda i, j: (0, 0))
    return pl.pallas_call(
        functools.partial(_dsa_kernel, k_sel=k_sel, idx_bits=idx_bits),
        grid=(b, nqg),
        in_specs=[tok(iq.shape[1]), tok(LANES), tok(qb.shape[1]), kspec, kspec, kspec,
                  pl.BlockSpec((tq, 1), lambda i, j: (q0 + j, 0)), row, row],
        out_specs=pl.BlockSpec((tq, qb.shape[1]), lambda i, j: (i * nqg + j, 0)),
        out_shape=jax.ShapeDtypeStruct((b * nqg * tq, qb.shape[1]), BF16),
        scratch_shapes=[pltpu.VMEM((tq, s), jnp.int32), pltpu.VMEM((tq, s), F32)],
        compiler_params=_cparams("arbitrary", "arbitrary"),
        name="dsa_attn",
    )(iq, iw, qb, ik2, bk2, bv2, qc, kc, kidx)


def _outproj_kernel(x_ref, oa_ref, ob_ref, ga_ref, gb_ref, woa_ref, wob_ref, wout_ref, ln2_ref,
                    y_ref, t_ref):
    ya = jnp.dot(oa_ref[...], woa_ref[...], preferred_element_type=F32)
    yb = jnp.dot(ob_ref[...], wob_ref[...], preferred_element_type=F32)
    mg = ga_ref[...] * ya + gb_ref[...] * yb
    y = x_ref[...] + jnp.dot(mg.astype(BF16), wout_ref[...], preferred_element_type=F32)
    y_ref[...] = y
    t = y * lax.rsqrt(jnp.mean(y * y, axis=-1, keepdims=True) + EPS) * ln2_ref[...]
    t_ref[...] = t.astype(BF16)


def _out_proj(x, oa, ob, ga, gb, w, tm):
    n, d = x.shape
    tok = lambda a: pl.BlockSpec((tm, a.shape[1]), lambda i: (i, 0))
    full = lambda a: pl.BlockSpec(a.shape, lambda i: (0,) * a.ndim)
    return pl.pallas_call(
        _outproj_kernel,
        grid=(n // tm,),
        in_specs=[tok(x), tok(oa), tok(ob), tok(ga), tok(gb),
                  full(w['woa']), full(w['wob']), full(w['wout']), full(w['ln2'])],
        out_specs=[tok(x), tok(x)],
        out_shape=[jax.ShapeDtypeStruct((n, d), F32), jax.ShapeDtypeStruct((n, d), BF16)],
        compiler_params=_cparams("arbitrary"),
        name="out_proj",
    )(x, oa, ob, ga, gb, w['woa'], w['wob'], w['wout'], w['ln2'])


def _top_rows(s, k, with_rank=False):
    t = s.shape[1]
    row = lax.broadcasted_iota(jnp.int32, (k, t), 0).astype(F32)
    out = jnp.zeros((k, t), F32)
    seen = jnp.zeros((1, t), F32)
    rank = jnp.full(s.shape, float(k), F32)
    for _ in range(k):
        m = jnp.max(s, axis=0, keepdims=True)
        eq = s == m
        if with_rank:
            rank = jnp.where(eq, seen, rank)
        upto = seen + jnp.sum(jnp.where(eq, 1.0, 0.0), axis=0, keepdims=True)
        out = jnp.where(jnp.logical_and(row >= seen, row < upto), m, out)
        seen = upto
        s = jnp.where(eq, -jnp.inf, s)
    return (out, rank) if with_rank else out


def _top_rows_distinct(s, k, with_rank):
    t = s.shape[1]
    row = lax.broadcasted_iota(jnp.int32, (k, t), 0)
    out = jnp.zeros((k, t), F32)
    rank = jnp.full(s.shape, float(k), F32)
    for q in range(k):
        m = jnp.max(s, axis=0, keepdims=True)
        eq = s == m
        if with_rank:
            rank = jnp.where(eq, float(q), rank)
        out = jnp.where(row == q, m, out)
        s = jnp.where(eq, -jnp.inf, s)
    used = jnp.sum(jnp.where(s == -jnp.inf, 1.0, 0.0), axis=0, keepdims=True)
    return out, rank, used


def _top_rows_into(s, k, n_inf, out_s, rank_s=None):
    out, rank, used = _top_rows_distinct(s, k, rank_s is not None)
    out_s[...] = out
    if rank_s is not None:
        rank_s[...] = rank

    @pl.when(jnp.max(used) > float(k + n_inf))
    def _():
        if rank_s is None:
            out_s[...] = _top_rows(s, k)
        else:
            out_s[...], rank_s[...] = _top_rows(s, k, with_rank=True)


def _pair_sum_candidates(a, b):
    k = PEER_TOPK
    assert k == 16 and a.shape[0] == k
    row = lax.broadcasted_iota(jnp.int32, (8, a.shape[1]), 0)
    blocks = [a[0:1] + b]
    n_inf = 0
    for i in range(1, 8):
        blocks.append(jnp.where(row < k // (i + 1), a[i:i + 1] + b[:8], -jnp.inf))
        n_inf += 8 - k // (i + 1)
    blocks.append(a[8:] + b[0:1])
    return jnp.concatenate(blocks, axis=0), n_inf


def _peer_kernel(t_ref, y_ref, wqt_ref, sk1_ref, sk2_ref, u_ref, vt_ref, o_ref,
                 m_s, r_s, a_s, b_s, gw_s, acc_s, qt_s, t1_s, t2_s, tp_s, rk_s):
    c = pl.program_id(1)
    nc = pl.num_programs(1) - 1
    te = u_ref.shape[0]
    half = PEER_DKEY // 2

    @pl.when(c == 0)
    def _route():
        qt = lax.dot_general(wqt_ref[...], t_ref[...], NT_DIMS, preferred_element_type=F32)
        qt_s[...] = qt.astype(BF16)

        @pl.loop(0, PEER_HEADS)
        def _head(hd):
            row0 = pl.multiple_of(hd * PEER_DKEY, PEER_DKEY)
            q1 = qt_s[pl.ds(row0, half), :]
            q2 = qt_s[pl.ds(row0 + half, half), :]
            s1 = jnp.dot(sk1_ref[hd], q1, preferred_element_type=F32)
            s2 = jnp.dot(sk2_ref[hd], q2, preferred_element_type=F32)
            _top_rows_into(s1, PEER_TOPK, 0, t1_s)
            _top_rows_into(s2, PEER_TOPK, 0, t2_s, rk_s)
            top1, top2, r = t1_s[...], t2_s[...], rk_s[...]
            cand, n_inf = _pair_sum_candidates(top1, top2)
            _top_rows_into(cand, PEER_TOPK, n_inf, tp_s)
            top = tp_s[...]
            tau = top[PEER_TOPK - 1:PEER_TOPK]
            z = jnp.sum(jnp.exp(top - top[0:1]), axis=0, keepdims=True)
            m = jnp.zeros(s1.shape, F32)
            for j in range(PEER_TOPK):
                reach = top1 + top2[j:j + 1] >= tau
                c_j = jnp.min(jnp.where(reach, top1, jnp.inf), axis=0, keepdims=True)
                m = m + jnp.where(s1 >= c_j, 1.0, 0.0)
            m_s[hd] = m
            r_s[hd] = r.astype(BF16)
            a_s[hd] = jnp.exp(s1 - top1[0:1])
            b_s[hd] = (jnp.exp(s2 - top2[0:1]) * (1.0 / z)).astype(BF16)
        gw_s[...] = jnp.zeros_like(gw_s)
        acc_s[...] = jnp.zeros_like(acc_s)

    acc_s[...] += jnp.dot(vt_ref[...], gw_s[...], preferred_element_type=F32)
    at = lax.dot_general(u_ref[...], t_ref[...], NT_DIMS, preferred_element_type=F32)
    zero = jnp.zeros((), BF16)
    tm = t_ref.shape[0]
    pk = (N_KEYS // BF16_ROWS, BF16_ROWS, tm)
    lw = min(LANES, tm)
    tok = jnp.zeros((BF16_ROWS, lw), F32)
    for k in range(te // N_KEYS):
        i1 = jnp.minimum(c, nc - 1) * (te // N_KEYS) + k
        w = None
        for hd in range(PEER_HEADS):
            hold = jnp.tile(tok * 0.0, (1, tm // lw))
            m_row = (jnp.broadcast_to(m_s[hd, pl.ds(i1, 1), :], pk[1:]) + hold).astype(BF16)
            a_row = jnp.broadcast_to(a_s[hd, pl.ds(i1, 1), :], pk[1:]).astype(BF16)
            sel = r_s[hd].reshape(pk) < m_row[None]
            term = jnp.where(sel, b_s[hd].reshape(pk) * a_row[None], zero)
            tok = tok + term[0, :, :lw].astype(F32)
            if (k * PEER_HEADS + hd) % PEER_LINK_EVERY == PEER_LINK_EVERY - 1:
                tok = pltpu.roll(tok, 1, 1)
            w = term if w is None else w + term
        g = jax.nn.gelu(at[k * N_KEYS:(k + 1) * N_KEYS])
        gw_s[k * N_KEYS:(k + 1) * N_KEYS, :] = w.reshape(N_KEYS, tm) * g.astype(BF16)

    @pl.when(c == nc)
    def _finish():
        o_ref[...] = y_ref[...] + acc_s[...].T


def _peer(t, y, w, tm, te):
    n, d = y.shape
    nc = w['u'].shape[0] // te
    full = lambda a: pl.BlockSpec(a.shape, lambda i, c: (0,) * a.ndim)
    tok = pl.BlockSpec((tm, d), lambda i, c: (i, 0))
    hk = (PEER_HEADS, N_KEYS, tm)
    return pl.pallas_call(
        _peer_kernel,
        grid=(n // tm, nc + 1),
        in_specs=[tok, tok, full(w['wqt']), full(w['sk1']), full(w['sk2']),
                  pl.BlockSpec((te, d), lambda i, c: (jnp.minimum(c, nc - 1), 0)),
                  pl.BlockSpec((d, te), lambda i, c: (0, jnp.maximum(c - 1, 0)))],
        out_specs=tok,
        out_shape=jax.ShapeDtypeStruct((n, d), F32),
        scratch_shapes=[pltpu.VMEM(hk, F32), pltpu.VMEM(hk, BF16), pltpu.VMEM(hk, F32),
                        pltpu.VMEM(hk, BF16), pltpu.VMEM((te, tm), BF16), pltpu.VMEM((d, tm), F32),
                        pltpu.VMEM((PEER_HEADS * PEER_DKEY, tm), BF16),
                        pltpu.VMEM((PEER_TOPK, tm), F32), pltpu.VMEM((PEER_TOPK, tm), F32),
                        pltpu.VMEM((PEER_TOPK, tm), F32), pltpu.VMEM((N_KEYS, tm), F32)],
        compiler_params=_cparams("arbitrary", "arbitrary"),
        name="peer",
    )(t, y, w['wqt'], w['sk1'], w['sk2'], w['u'], w['vt'])


def _prep_weights(ln1_g, w_in, a_q_norm_g, a_kv_norm_g, a_w_uq, a_w_ukv, a_qk_g, b_qk_g,
                  w_o_a, w_o_b, w_out, ln2_g, peer_wq, peer_subkeys, peer_u, peer_v):
    d = w_in.shape[0]
    offs = np.cumsum(IN_SPLITS).tolist()
    cq, ckv, kpe, bq, bk, bv, iq, iw, ik, gates = jnp.split(w_in, offs, axis=1)
    z = lambda n: jnp.zeros((d, n), w_in.dtype)
    win = jnp.concatenate([
        cq, ckv, z(A_NOPE), kpe, z(LANES - A_NOPE - A_ROPE), bq,
        bk, z(LANES - B_HEAD_DIM), bv, z(LANES - B_HEAD_DIM), iq,
        iw, z(LANES - IDX_HEADS), ik, z(LANES - IDX_DIM)], axis=1).astype(BF16)
    assert win.shape[1] == N_SLAB_SMALL * LANES

    def pad_last(a, n):
        return jnp.pad(a, [(0, 0)] * (a.ndim - 1) + [(0, n - a.shape[-1])])

    wuq = pad_last(a_w_uq.reshape(A_QRANK, A_HEADS, A_NOPE + A_ROPE), LANES)
    wuq = wuq.reshape(A_QRANK, A_HEADS * LANES).astype(BF16)
    ukv = a_w_ukv.reshape(A_KVRANK, A_HEADS, A_NOPE + A_V)
    wk = pad_last(ukv[..., :A_NOPE], LANES).reshape(A_KVRANK, A_HEADS * LANES).astype(BF16)
    wv = pad_last(ukv[..., A_NOPE:], LANES).reshape(A_KVRANK, A_HEADS * LANES).astype(BF16)
    woa = pad_last(w_o_a.reshape(A_HEADS, A_V, d).transpose(0, 2, 1), LANES)
    woa = woa.transpose(0, 2, 1).reshape(A_HEADS * LANES, d).astype(BF16)

    row = lambda v: pad_last(v, LANES)[None, :]
    gs = jnp.concatenate([
        row(a_qk_g[0]),
        row(jnp.concatenate([jnp.zeros((A_NOPE,), F32), a_qk_g[1, A_NOPE:]])),
        row(jnp.concatenate([b_qk_g[0], b_qk_g[0]])),
        row(b_qk_g[1]),
        jnp.zeros((4, LANES), F32)], axis=0)
    return dict(
        ln1=ln1_g[None, :], win=win, wg=gates.astype(BF16), gq=a_q_norm_g[None, :],
        gkv=a_kv_norm_g[None, :], wuq=wuq, gs=gs, wk=wk, wv=wv, gk=row(a_qk_g[1, :A_NOPE]),
        woa=woa, wob=w_o_b.astype(BF16), wout=w_out.astype(BF16), ln2=ln2_g[None, :],
        wqt=peer_wq.T.astype(BF16),
        sk1=peer_subkeys[:, 0].astype(BF16), sk2=peer_subkeys[:, 1].astype(BF16),
        u=peer_u.astype(BF16), vt=peer_v.T.astype(BF16))


def _key_layout(meta, rows):
    pad = jnp.zeros((meta.shape[0], LANES - meta.shape[1], meta.shape[2]), meta.dtype)
    return jnp.concatenate([meta, pad, rows], axis=1)


def _pair_slabs(k):
    kb = k.astype(BF16)
    z = jnp.zeros_like(kb)
    return jnp.stack([jnp.concatenate([kb, z], -1), jnp.concatenate([z, kb], -1)], axis=1)


def _mixers_and_peer(x, rows, keys, qc, kc, kidx, k_sel, w, tq, tm, te, groups):
    lat_k, kpes_k, bk_k, bv_k, ik_k = keys
    (_, _, _, _, _, qa, qb, iq, iw, ga, gb) = rows
    b = lat_k.shape[0]
    ka, va = _kv_prep(lat_k, kpes_k, w)
    ik2, bk2, bv2 = _pair_slabs(ik_k), _pair_slabs(bk_k), _pair_slabs(bv_k)
    oa, ob = [], []
    for q0, nqg, s in groups:
        o = _mla_attn(qa, ka, va, qc, kc, tq, q0, nqg, s)
        oa.append(o.reshape(b, nqg * tq, o.shape[-1]))
        o = _dsa_attn(iq, iw, qb, ik2, bk2, bv2, qc, kc, kidx, tq, k_sel, q0, nqg, s)
        ob.append(o.reshape(b, nqg * tq, o.shape[-1]))
    oa = jnp.concatenate(oa, axis=1).reshape(x.shape[0], -1)
    ob = jnp.concatenate(ob, axis=1).reshape(x.shape[0], -1)
    y1, t2 = _out_proj(x, oa, ob, ga, gb, w, tm)
    tp = 2 * tm if x.shape[0] % (2 * tm) == 0 else tm
    return _peer(t2, y1, w, tp, te)


def kernel(x_prompt, x_sample, cache_a_latent, cache_a_kpe, cache_b_k, cache_b_v, cache_b_idx_k,
           meta_tokens, ln1_g, w_in, a_q_norm_g, a_kv_norm_g, a_w_uq, a_w_ukv, a_qk_g, b_qk_g,
           w_o_a, w_o_b, w_out, ln2_g, peer_wq, peer_subkeys, peer_u, peer_v):
    bp, sp, d = x_prompt.shape
    bs, ts, _ = x_sample.shape
    depth, _, past, _ = cache_a_latent.shape
    assert depth == 1 and sp % CHUNK == 0
    w = _prep_weights(ln1_g[0], w_in[0], a_q_norm_g[0], a_kv_norm_g[0], a_w_uq[0], a_w_ukv[0],
                      a_qk_g[0], b_qk_g[0], w_o_a[0], w_o_b[0], w_out[0], ln2_g[0], peer_wq[0],
                      peer_subkeys[0], peer_u[0], peer_v[0])
    tm = 256
    te = 512
    i32 = jnp.int32

    xq = x_prompt.reshape(bp * sp, d)
    rows_p = _in_proj(xq, _all_tables(N_META + jnp.arange(sp, dtype=i32)), bp, tm, w)
    rows_m = _in_proj(meta_tokens.astype(F32), _all_tables(jnp.arange(N_META, dtype=i32)), 1, N_META, w)
    xs = x_sample.reshape(bs * ts, d)
    pos_s = past + jnp.arange(ts, dtype=i32)
    rows_s = _in_proj(xs, _all_tables(jnp.tile(pos_s, bs)), 1, bs * ts, w)

    s_p = LANES + sp
    per_b = lambda a, b, t: a.reshape(b, t, a.shape[-1])
    meta_b = lambda a: jnp.broadcast_to(a[None], (bp,) + a.shape)
    keys_p = tuple(_key_layout(meta_b(m), per_b(r, bp, sp)) for r, m in zip(rows_p[:5], rows_m[:5]))
    chunk_q = jnp.arange(sp, dtype=i32) // CHUNK
    n_pad = LANES - N_META
    kc_p = jnp.concatenate([jnp.full((N_META,), -1, i32), jnp.full((n_pad,), PAD_CHUNK, i32),
                            chunk_q])[None, :]
    kidx_p = jnp.concatenate([jnp.arange(N_META, dtype=i32), s_p + jnp.arange(n_pad, dtype=i32),
                              N_META + jnp.arange(sp, dtype=i32)])[None, :]
    tq = 256
    nq = sp // tq
    tpg = max(1, nq // PROMPT_GROUPS)
    groups_p = [(q0, min(tpg, nq - q0), LANES + min(q0 + tpg, nq) * tq) for q0 in range(0, nq, tpg)]
    y_p = _mixers_and_peer(xq, rows_p, keys_p, chunk_q[:, None], kc_p, kidx_p,
                           min(IDX_TOPK, sp // 4), w, tq, tm, te, groups_p)

    s_s = _round_up(past + ts, LANES)
    kpes_cache = jnp.pad(cache_a_kpe[0], ((0, 0), (0, 0), (A_NOPE, LANES - A_NOPE - A_ROPE)))
    caches = (cache_a_latent[0], kpes_cache, cache_b_k[0], cache_b_v[0], cache_b_idx_k[0])
    keys_s = tuple(jnp.pad(jnp.concatenate([c.astype(F32), per_b(r, bs, ts)], axis=1),
                           ((0, 0), (0, s_s - past - ts), (0, 0)))
                   for c, r in zip(caches, rows_s[:5]))
    kc_s = jnp.concatenate([jnp.zeros((past,), i32), jnp.ones((ts,), i32),
                            jnp.full((s_s - past - ts,), PAD_CHUNK, i32)])[None, :]
    kidx_s = jnp.arange(s_s, dtype=i32)[None, :]
    y_s = _mixers_and_peer(xs, rows_s, keys_s, jnp.ones((ts, 1), i32), kc_s, kidx_s,
                           min(IDX_TOPK, (past + ts) // 4), w, ts, bs * ts, te, [(0, 1, s_s)])

    kpe_of = lambda r: r[:, A_NOPE:A_NOPE + A_ROPE]

    def new_rows_p(r, m):
        return jnp.concatenate([meta_b(m), per_b(r, bp, sp)], axis=1)[None]

    outs_p = [new_rows_p(rows_p[0], rows_m[0]), new_rows_p(kpe_of(rows_p[1]), kpe_of(rows_m[1]))]
    outs_p += [new_rows_p(rows_p[i], rows_m[i]) for i in (2, 3, 4)]
    outs_s = [per_b(rows_s[0], bs, ts)[None], per_b(kpe_of(rows_s[1]), bs, ts)[None]]
    outs_s += [per_b(rows_s[i], bs, ts)[None] for i in (2, 3, 4)]
    return (y_p.reshape(bp, sp, d), y_s.reshape(bs, ts, d), *outs_p, *outs_s)
```

```python
import functools

import jax
import jax.numpy as jnp
import numpy as np
from jax import lax
from jax.experimental import pallas as pl
from jax.experimental.pallas import tpu as pltpu

F32 = jnp.float32
BF16 = jnp.bfloat16

LANES = 128
BF16_ROWS = 16
VMEM_LIMIT = 52 * 1024 * 1024

CHUNK = 64
N_META = 16
ROPE_THETA = 500000.0
EPS = 1e-6

A_HEADS = 8
A_NOPE = 64
A_ROPE = 32
A_V = 64
A_QRANK = 256
A_KVRANK = 128
A_SCALE = (A_NOPE + A_ROPE) ** -0.5
MLA_HEADS_PER_STEP = 2

B_HEADS = 8
B_HEAD_DIM = 64
B_ROT = B_HEAD_DIM // 4
B_SCALE = B_HEAD_DIM ** -0.5
IDX_HEADS = 8
IDX_DIM = 64
IDX_ROT = IDX_DIM // 4
IDX_TOPK = 256
IDX_W_SCALE = (IDX_HEADS * IDX_DIM) ** -0.5

PEER_HEADS = 8
PEER_DKEY = 128
N_KEYS = 128
PEER_TOPK = 16
PEER_LINK_EVERY = 2
PROMPT_GROUPS = 16

IN_SPLITS = (A_QRANK, A_KVRANK, A_ROPE, B_HEADS * B_HEAD_DIM, B_HEAD_DIM, B_HEAD_DIM,
             IDX_HEADS * IDX_DIM, IDX_HEADS, IDX_DIM)

NEG_BIG = -1e30
LOG2_E = 1.4426950408889634
INT_MIN = -2147483648
PAD_CHUNK = 1 << 30

NT_DIMS = (((1,), (1,)), ((), ()))


def _cparams(*sem):
    return pltpu.CompilerParams(dimension_semantics=sem, vmem_limit_bytes=VMEM_LIMIT)


def _round_up(n, m):
    return -(-n // m) * m


def _pick_tile(n, target):
    best = LANES
    for t in range(LANES, min(n, target) + 1, LANES):
        if n % t == 0:
            best = t
    return best


def _rot_tables(pos, rot_dim, offsets):
    half = rot_dim // 2
    inv = ROPE_THETA ** (-jnp.arange(half, dtype=F32) * (2.0 / rot_dim))
    freq_idx = np.zeros((LANES,), np.int32)
    first = np.zeros((LANES,), bool)
    second = np.zeros((LANES,), bool)
    for o in offsets:
        freq_idx[o:o + rot_dim] = np.arange(rot_dim) % half
        first[o:o + half] = True
        second[o + half:o + rot_dim] = True
    ang = pos.astype(F32)[:, None] * inv[freq_idx][None, :]
    cos, sin = jnp.cos(ang), jnp.sin(ang)
    c = jnp.where((first | second)[None, :], cos, 1.0)
    sm = jnp.where(first[None, :], -sin, 0.0)
    sp = jnp.where(second[None, :], sin, 0.0)
    return jnp.stack([c, sm, sp])


def _all_tables(pos):
    return jnp.concatenate([
        _rot_tables(pos, A_ROPE, (A_NOPE,)),
        _rot_tables(pos, B_ROT, (0, B_HEAD_DIM)),
        _rot_tables(pos, B_ROT, (0,)),
    ])


def _rot(x, tab_ref, base, half):
    c = tab_ref[base]
    sm = tab_ref[base + 1]
    sp = tab_ref[base + 2]
    return x * c + pltpu.roll(x, LANES - half, 1) * sm + pltpu.roll(x, half, 1) * sp


SLAB_CQ, SLAB_CKV, SLAB_KPE, SLAB_BQ, SLAB_BK, SLAB_BV = 0, 2, 3, 4, 8, 9
SLAB_IQ, SLAB_IW, SLAB_IK, SLAB_GA, N_SLAB_SMALL = 10, 14, 15, 16, 16


GATE_COLS = 512


def _cols(s0, n=1):
    return slice(s0 * LANES, (s0 + n) * LANES)


def _inproj_kernel(x_ref, ln1_ref, win_ref, wg_ref, gq_ref, gkv_ref, wuq_ref, gs_ref, tab_ref,
                   lat_ref, kpes_ref, bk_ref, bv_ref, ik_ref, qa_ref, qb_ref, iq_ref, iw_ref,
                   ga_ref, gb_ref):
    d = x_ref.shape[1]
    x = x_ref[...]
    h = x * lax.rsqrt(jnp.mean(x * x, axis=-1, keepdims=True) + EPS) * ln1_ref[...]
    hb = h.astype(BF16)
    lane = lax.broadcasted_iota(jnp.int32, (1, LANES), 1)
    lo = lane < B_HEAD_DIM

    def proj(s0, n=1):
        return jnp.dot(hb, win_ref[:, _cols(s0, n)], preferred_element_type=F32)

    def seg_rs(x2, mask, n):
        return lax.rsqrt(jnp.sum(jnp.where(mask, x2, 0.0), axis=-1, keepdims=True) * (1.0 / n) + EPS)

    cq = proj(SLAB_CQ, 2)
    cq = cq * lax.rsqrt(jnp.mean(cq * cq, axis=-1, keepdims=True) + EPS) * gq_ref[...]
    q = jnp.dot(cq.astype(BF16), wuq_ref[...], preferred_element_type=F32)
    rope_m = jnp.logical_and(lane >= A_NOPE, lane < A_NOPE + A_ROPE)
    for hd in range(A_HEADS):
        s = q[:, _cols(hd)]
        s2 = s * s
        sc = jnp.where(lo, seg_rs(s2, lo, A_NOPE), seg_rs(s2, rope_m, A_ROPE))
        qa_ref[:, _cols(hd)] = _rot(s * sc * gs_ref[0:1, :], tab_ref, 0, A_ROPE // 2).astype(BF16)

    both = proj(SLAB_CKV, 2)
    ckv = both[:, _cols(0)]
    lat_ref[...] = ckv * lax.rsqrt(jnp.mean(ckv * ckv, axis=-1, keepdims=True) + EPS) * gkv_ref[...]
    kp = both[:, _cols(1)]
    kp = kp * seg_rs(kp * kp, rope_m, A_ROPE) * gs_ref[1:2, :]
    kpes_ref[...] = _rot(kp, tab_ref, 0, A_ROPE // 2)

    bq = proj(SLAB_BQ, B_HEADS // 2)
    for j in range(B_HEADS // 2):
        s = bq[:, _cols(j)]
        s2 = s * s
        sc = jnp.where(lo, seg_rs(s2, lo, B_HEAD_DIM), seg_rs(s2, jnp.logical_not(lo), B_HEAD_DIM))
        s = _rot(s * sc * gs_ref[2:3, :], tab_ref, 3, B_ROT // 2) * B_SCALE
        qb_ref[:, _cols(j)] = s.astype(BF16)
    both = proj(SLAB_BK, 2)
    s = both[:, _cols(0)]
    s = s * seg_rs(s * s, lo, B_HEAD_DIM) * gs_ref[3:4, :]
    bk_ref[...] = _rot(s, tab_ref, 6, B_ROT // 2)[:, :B_HEAD_DIM]
    bv_ref[...] = both[:, LANES:LANES + B_HEAD_DIM]

    iq = proj(SLAB_IQ, IDX_HEADS // 2)
    for j in range(IDX_HEADS // 2):
        iq_ref[:, _cols(j)] = _rot(iq[:, _cols(j)], tab_ref, 3, IDX_ROT // 2).astype(BF16)
    both = proj(SLAB_IW, 2)
    iw_ref[...] = both[:, _cols(0)] * IDX_W_SCALE
    ik_ref[...] = _rot(both[:, _cols(1)], tab_ref, 6, IDX_ROT // 2)[:, :IDX_DIM]

    for j in range(0, d, GATE_COLS):
        ga_ref[:, j:j + GATE_COLS] = jax.nn.sigmoid(
            jnp.dot(hb, wg_ref[:, j:j + GATE_COLS], preferred_element_type=F32))
        gb_ref[:, j:j + GATE_COLS] = jax.nn.sigmoid(
            jnp.dot(hb, wg_ref[:, d + j:d + j + GATE_COLS], preferred_element_type=F32))


def _in_proj(x, tabs, reps, tm, w):
    n, d = x.shape
    r = tabs.shape[1]
    npb = r // tm
    assert r % tm == 0 and n == reps * r
    tok = lambda width: pl.BlockSpec((tm, width), lambda p, b: (b * npb + p, 0))
    full = lambda a: pl.BlockSpec(a.shape, lambda p, b: (0,) * a.ndim)
    outs = [
        ((n, A_KVRANK), F32), ((n, LANES), F32), ((n, B_HEAD_DIM), F32), ((n, B_HEAD_DIM), F32),
        ((n, IDX_DIM), F32), ((n, A_HEADS * LANES), BF16), ((n, B_HEADS * B_HEAD_DIM), BF16),
        ((n, IDX_HEADS * IDX_DIM), BF16), ((n, LANES), F32), ((n, d), F32), ((n, d), F32),
    ]
    return pl.pallas_call(
        _inproj_kernel,
        grid=(npb, reps),
        in_specs=[tok(d), full(w['ln1']), full(w['win']), full(w['wg']), full(w['gq']), full(w['gkv']),
                  full(w['wuq']), full(w['gs']),
                  pl.BlockSpec((tabs.shape[0], tm, LANES), lambda p, b: (0, p, 0))],
        out_specs=[tok(s[1]) for s, _ in outs],
        out_shape=[jax.ShapeDtypeStruct(s, t) for s, t in outs],
        compiler_params=_cparams("arbitrary", "arbitrary"),
        name="in_proj",
    )(x, w['ln1'], w['win'], w['wg'], w['gq'], w['gkv'], w['wuq'], w['gs'], tabs)


def _kvprep_kernel(lat_ref, kpes_ref, wk_ref, wv_ref, gk_ref, ka_ref, va_ref):
    lat = lat_ref[...].astype(BF16)
    kk = jnp.dot(lat, wk_ref[...], preferred_element_type=F32)
    vv = jnp.dot(lat, wv_ref[...], preferred_element_type=F32)
    kpes = kpes_ref[...]
    for hd in range(A_HEADS):
        s = kk[:, _cols(hd)]
        rs = lax.rsqrt(jnp.sum(s * s, axis=-1, keepdims=True) * (1.0 / A_NOPE) + EPS)
        ka_ref[hd] = (s * rs * gk_ref[...] + kpes).astype(BF16)
        va_ref[hd] = vv[:, _cols(hd)].astype(BF16)


def _kv_prep(lat_keys, kpes_keys, w):
    b, s, _ = lat_keys.shape
    ts = _pick_tile(s, 512)
    row = pl.BlockSpec((None, ts, LANES), lambda i, j: (i, j, 0))
    full = lambda a: pl.BlockSpec(a.shape, lambda i, j: (0,) * a.ndim)
    hd = pl.BlockSpec((None, A_HEADS, ts, LANES), lambda i, j: (i, 0, j, 0))
    shp = jax.ShapeDtypeStruct((b, A_HEADS, s, LANES), BF16)
    return pl.pallas_call(
        _kvprep_kernel,
        grid=(b, s // ts),
        in_specs=[row, row, full(w['wk']), full(w['wv']), full(w['gk'])],
        out_specs=[hd, hd],
        out_shape=[shp, shp],
        compiler_params=_cparams("arbitrary", "arbitrary"),
        name="kv_prep",
    )(lat_keys, kpes_keys, w['wk'], w['wv'], w['gk'])


def _mla_kernel(q_ref, k_ref, v_ref, qc_ref, kc_ref, o_ref, bias_s):
    j = pl.program_id(2)

    @pl.when(pl.program_id(1) == 0)
    def _():
        bias_s[j] = jnp.where(kc_ref[...] <= qc_ref[...], 0.0, NEG_BIG)

    for hd in range(MLA_HEADS_PER_STEP):
        s = lax.dot_general(q_ref[:, _cols(hd)], k_ref[hd], NT_DIMS, preferred_element_type=F32)
        s = s + bias_s[j]
        p = jnp.exp2((s - jnp.max(s, axis=-1, keepdims=True)) * (A_SCALE * LOG2_E))
        inv_l = 1.0 / jnp.sum(p, axis=-1, keepdims=True)
        o = jnp.dot(p.astype(BF16), v_ref[hd], preferred_element_type=F32)
        o_ref[:, _cols(hd)] = (o * inv_l).astype(BF16)


def _mla_attn(qa, ka, va, qc, kc, tq, q0, nqg, s):
    n = qa.shape[0]
    b = ka.shape[0]
    nq = n // (b * tq)
    hps = MLA_HEADS_PER_STEP
    qspec = pl.BlockSpec((tq, hps * LANES), lambda i, h, j: (i * nq + q0 + j, h))
    kspec = pl.BlockSpec((None, hps, s, LANES), lambda i, h, j: (i, h, 0, 0))
    return pl.pallas_call(
        _mla_kernel,
        grid=(b, A_HEADS // hps, nqg),
        in_specs=[qspec, kspec, kspec,
                  pl.BlockSpec((tq, 1), lambda i, h, j: (q0 + j, 0)),
                  pl.BlockSpec((1, s), lambda i, h, j: (0, 0))],
        out_specs=pl.BlockSpec((tq, hps * LANES), lambda i, h, j: (i * nqg + j, h)),
        out_shape=jax.ShapeDtypeStruct((b * nqg * tq, A_HEADS * LANES), BF16),
        scratch_shapes=[pltpu.VMEM((nqg, tq, s), F32)],
        compiler_params=_cparams("arbitrary", "arbitrary", "arbitrary"),
        name="mla_attn",
    )(qa, ka, va, qc, kc)


def _dsa_kernel(iq_ref, iw_ref, qb_ref, ik2_ref, bk2_ref, bv2_ref, qc_ref, kc_ref, kidx_ref,
                o_ref, key_s, bias_s, *, k_sel, idx_bits):
    tq = key_s.shape[0]
    adm = kc_ref[...] <= qc_ref[...]
    iw = iw_ref[...]
    score = None
    for j in range(IDX_HEADS // 2):
        iqj = iq_ref[:, _cols(j)]
        for half in range(2):
            hd = 2 * j + half
            rel = jnp.maximum(
                lax.dot_general(iqj, ik2_ref[half], NT_DIMS, preferred_element_type=F32), 0.0)
            term = iw[:, hd:hd + 1] * rel
            score = term if score is None else score + term

    bits = lax.bitcast_convert_type(score, jnp.int32)
    key = jnp.where(bits < 0, bits ^ jnp.int32(0x7FFFFFFF), bits)
    key = jnp.where(key == -1, 0, key)
    key_s[...] = jnp.where(adm, key, INT_MIN)

    kf = float(k_sel)

    def count(mask):
        return jnp.sum(jnp.where(mask, 1.0, 0.0), axis=-1, keepdims=True)

    def value_step(i, cur):
        cand = cur + lax.shift_left(jnp.int32(1), 31 - i)
        return jnp.where(count(key_s[...] >= cand) >= kf, cand, cur)

    tau = lax.fori_loop(0, 32, value_step, jnp.full((tq, 1), INT_MIN, jnp.int32))
    key = key_s[...]
    ge = key >= tau
    c_ge = count(ge)
    bias_s[...] = jnp.where(jnp.logical_and(ge, adm), 0.0, NEG_BIG)
    amb = jnp.logical_and(tau > INT_MIN, c_ge > kf)

    @pl.when(jnp.max(jnp.where(amb, 1.0, 0.0)) > 0.0)
    def _():
        key = key_s[...]
        gt = key > tau
        tie = key == tau
        need = kf - count(gt)
        kidx = kidx_ref[...]

        def index_step(i, cur):
            cand = cur + lax.shift_left(jnp.int32(1), idx_bits - 1 - i)
            below = count(jnp.logical_and(tie, kidx < cand))
            return jnp.where(below < need, cand, cur)

        last = lax.fori_loop(0, idx_bits, index_step, jnp.zeros((tq, 1), jnp.int32))
        sel = jnp.logical_or(gt, jnp.logical_and(tie, kidx <= last))
        bias_s[...] = jnp.where(jnp.logical_and(sel, adm), 0.0, NEG_BIG)

    for j in range(B_HEADS // 2):
        qbj = qb_ref[:, _cols(j)]
        acc = None
        for half in range(2):
            lg = lax.dot_general(qbj, bk2_ref[half], NT_DIMS, preferred_element_type=F32)
            lg = lg + bias_s[...]
            p = jnp.exp(lg - jnp.max(lg, axis=-1, keepdims=True))
            inv_l = 1.0 / jnp.sum(p, axis=-1, keepdims=True)
            o = jnp.dot(p.astype(BF16), bv2_ref[half], preferred_element_type=F32) * inv_l
            acc = o if acc is None else acc + o
        o_ref[:, _cols(j)] = acc.astype(BF16)


def _dsa_attn(iq, iw, qb, ik2, bk2, bv2, qc, kc, kidx, tq, k_sel, q0, nqg, s):
    n = iq.shape[0]
    b, _, s_all, _ = ik2.shape
    nq = n // (b * tq)
    idx_bits = max(1, int(np.ceil(np.log2(s_all + N_META + 1))))
    tok = lambda width: pl.BlockSpec((tq, width), lambda i, j: (i * nq + q0 + j, 0))
    kspec = pl.BlockSpec((None, 2, s, LANES), lambda i, j: (i, 0, 0, 0))
    row = pl.BlockSpec((1, s), lambda i, j: (0, 0))
    return pl.pallas_call(
        functools.partial(_dsa_kernel, k_sel=k_sel, idx_bits=idx_bits),
        grid=(b, nqg),
        in_specs=[tok(iq.shape[1]), tok(LANES), tok(qb.shape[1]), kspec, kspec, kspec,
                  pl.BlockSpec((tq, 1), lambda i, j: (q0 + j, 0)), row, row],
        out_specs=pl.BlockSpec((tq, qb.shape[1]), lambda i, j: (i * nqg + j, 0)),
        out_shape=jax.ShapeDtypeStruct((b * nqg * tq, qb.shape[1]), BF16),
        scratch_shapes=[pltpu.VMEM((tq, s), jnp.int32), pltpu.VMEM((tq, s), F32)],
        compiler_params=_cparams("arbitrary", "arbitrary"),
        name="dsa_attn",
    )(iq, iw, qb, ik2, bk2, bv2, qc, kc, kidx)


def _outproj_kernel(x_ref, oa_ref, ob_ref, ga_ref, gb_ref, woa_ref, wob_ref, wout_ref, ln2_ref,
                    y_ref, t_ref):
    ya = jnp.dot(oa_ref[...], woa_ref[...], preferred_element_type=F32)
    yb = jnp.dot(ob_ref[...], wob_ref[...], preferred_element_type=F32)
    mg = ga_ref[...] * ya + gb_ref[...] * yb
    y = x_ref[...] + jnp.dot(mg.astype(BF16), wout_ref[...], preferred_element_type=F32)
    y_ref[...] = y
    t = y * lax.rsqrt(jnp.mean(y * y, axis=-1, keepdims=True) + EPS) * ln2_ref[...]
    t_ref[...] = t.astype(BF16)


def _out_proj(x, oa, ob, ga, gb, w, tm):
    n, d = x.shape
    tok = lambda a: pl.BlockSpec((tm, a.shape[1]), lambda i: (i, 0))
    full = lambda a: pl.BlockSpec(a.shape, lambda i: (0,) * a.ndim)
    return pl.pallas_call(
        _outproj_kernel,
        grid=(n // tm,),
        in_specs=[tok(x), tok(oa), tok(ob), tok(ga), tok(gb),
                  full(w['woa']), full(w['wob']), full(w['wout']), full(w['ln2'])],
        out_specs=[tok(x), tok(x)],
        out_shape=[jax.ShapeDtypeStruct((n, d), F32), jax.ShapeDtypeStruct((n, d), BF16)],
        compiler_params=_cparams("arbitrary"),
        name="out_proj",
    )(x, oa, ob, ga, gb, w['woa'], w['wob'], w['wout'], w['ln2'])


def _top_rows(s, k, with_rank=False):
    t = s.shape[1]
    row = lax.broadcasted_iota(jnp.int32, (k, t), 0).astype(F32)
    out = jnp.zeros((k, t), F32)
    seen = jnp.zeros((1, t), F32)
    rank = jnp.full(s.shape, float(k), F32)
    for _ in range(k):
        m = jnp.max(s, axis=0, keepdims=True)
        eq = s == m
        if with_rank:
            rank = jnp.where(eq, seen, rank)
        upto = seen + jnp.sum(jnp.where(eq, 1.0, 0.0), axis=0, keepdims=True)
        out = jnp.where(jnp.logical_and(row >= seen, row < upto), m, out)
        seen = upto
        s = jnp.where(eq, -jnp.inf, s)
    return (out, rank) if with_rank else out


def _top_rows_distinct(s, k, with_rank):
    t = s.shape[1]
    row = lax.broadcasted_iota(jnp.int32, (k, t), 0)
    out = jnp.zeros((k, t), F32)
    rank = jnp.full(s.shape, float(k), F32)
    for q in range(k):
        m = jnp.max(s, axis=0, keepdims=True)
        eq = s == m
        if with_rank:
            rank = jnp.where(eq, float(q), rank)
        out = jnp.where(row == q, m, out)
        s = jnp.where(eq, -jnp.inf, s)
    used = jnp.sum(jnp.where(s == -jnp.inf, 1.0, 0.0), axis=0, keepdims=True)
    return out, rank, used


def _top_rows_into(s, k, n_inf, out_s, rank_s=None):
    out, rank, used = _top_rows_distinct(s, k, rank_s is not None)
    out_s[...] = out
    if rank_s is not None:
        rank_s[...] = rank

    @pl.when(jnp.max(used) > float(k + n_inf))
    def _():
        if rank_s is None:
            out_s[...] = _top_rows(s, k)
        else:
            out_s[...], rank_s[...] = _top_rows(s, k, with_rank=True)


def _pair_sum_candidates(a, b):
    k = PEER_TOPK
    assert k == 16 and a.shape[0] == k
    row = lax.broadcasted_iota(jnp.int32, (8, a.shape[1]), 0)
    blocks = [a[0:1] + b]
    n_inf = 0
    for i in range(1, 8):
        blocks.append(jnp.where(row < k // (i + 1), a[i:i + 1] + b[:8], -jnp.inf))
        n_inf += 8 - k // (i + 1)
    blocks.append(a[8:] + b[0:1])
    return jnp.concatenate(blocks, axis=0), n_inf


def _peer_kernel(t_ref, y_ref, wqt_ref, sk1_ref, sk2_ref, u_ref, vt_ref, o_ref,
                 m_s, r_s, a_s, b_s, gw_s, acc_s, qt_s, t1_s, t2_s, tp_s, rk_s):
    c = pl.program_id(1)
    nc = pl.num_programs(1) - 1
    te = u_ref.shape[0]
    half = PEER_DKEY // 2

    @pl.when(c == 0)
    def _route():
        qt = lax.dot_general(wqt_ref[...], t_ref[...], NT_DIMS, preferred_element_type=F32)
        qt_s[...] = qt.astype(BF16)

        @pl.loop(0, PEER_HEADS)
        def _head(hd):
            row0 = pl.multiple_of(hd * PEER_DKEY, PEER_DKEY)
            q1 = qt_s[pl.ds(row0, half), :]
            q2 = qt_s[pl.ds(row0 + half, half), :]
            s1 = jnp.dot(sk1_ref[hd], q1, preferred_element_type=F32)
            s2 = jnp.dot(sk2_ref[hd], q2, preferred_element_type=F32)
            _top_rows_into(s1, PEER_TOPK, 0, t1_s)
            _top_rows_into(s2, PEER_TOPK, 0, t2_s, rk_s)
            top1, top2, r = t1_s[...], t2_s[...], rk_s[...]
            cand, n_inf = _pair_sum_candidates(top1, top2)
            _top_rows_into(cand, PEER_TOPK, n_inf, tp_s)
            top = tp_s[...]
            tau = top[PEER_TOPK - 1:PEER_TOPK]
            z = jnp.sum(jnp.exp(top - top[0:1]), axis=0, keepdims=True)
            m = jnp.zeros(s1.shape, F32)
            for j in range(PEER_TOPK):
                reach = top1 + top2[j:j + 1] >= tau
                c_j = jnp.min(jnp.where(reach, top1, jnp.inf), axis=0, keepdims=True)
                m = m + jnp.where(s1 >= c_j, 1.0, 0.0)
            m_s[hd] = m
            r_s[hd] = r.astype(BF16)
            a_s[hd] = jnp.exp(s1 - top1[0:1])
            b_s[hd] = (jnp.exp(s2 - top2[0:1]) * (1.0 / z)).astype(BF16)
        gw_s[...] = jnp.zeros_like(gw_s)
        acc_s[...] = jnp.zeros_like(acc_s)

    acc_s[...] += jnp.dot(vt_ref[...], gw_s[...], preferred_element_type=F32)
    at = lax.dot_general(u_ref[...], t_ref[...], NT_DIMS, preferred_element_type=F32)
    zero = jnp.zeros((), BF16)
    tm = t_ref.shape[0]
    pk = (N_KEYS // BF16_ROWS, BF16_ROWS, tm)
    lw = min(LANES, tm)
    tok = jnp.zeros((BF16_ROWS, lw), F32)
    for k in range(te // N_KEYS):
        i1 = jnp.minimum(c, nc - 1) * (te // N_KEYS) + k
        w = None
        for hd in range(PEER_HEADS):
            hold = jnp.tile(tok * 0.0, (1, tm // lw))
            m_row = (jnp.broadcast_to(m_s[hd, pl.ds(i1, 1), :], pk[1:]) + hold).astype(BF16)
            a_row = jnp.broadcast_to(a_s[hd, pl.ds(i1, 1), :], pk[1:]).astype(BF16)
            sel = r_s[hd].reshape(pk) < m_row[None]
            term = jnp.where(sel, b_s[hd].reshape(pk) * a_row[None], zero)
            tok = tok + term[0, :, :lw].astype(F32)
            if (k * PEER_HEADS + hd) % PEER_LINK_EVERY == PEER_LINK_EVERY - 1:
                tok = pltpu.roll(tok, 1, 1)
            w = term if w is None else w + term
        g = jax.nn.gelu(at[k * N_KEYS:(k + 1) * N_KEYS])
        gw_s[k * N_KEYS:(k + 1) * N_KEYS, :] = w.reshape(N_KEYS, tm) * g.astype(BF16)

    @pl.when(c == nc)
    def _finish():
        o_ref[...] = y_ref[...] + acc_s[...].T


def _peer(t, y, w, tm, te):
    n, d = y.shape
    nc = w['u'].shape[0] // te
    full = lambda a: pl.BlockSpec(a.shape, lambda i, c: (0,) * a.ndim)
    tok = pl.BlockSpec((tm, d), lambda i, c: (i, 0))
    hk = (PEER_HEADS, N_KEYS, tm)
    return pl.pallas_call(
        _peer_kernel,
        grid=(n // tm, nc + 1),
        in_specs=[tok, tok, full(w['wqt']), full(w['sk1']), full(w['sk2']),
                  pl.BlockSpec((te, d), lambda i, c: (jnp.minimum(c, nc - 1), 0)),
                  pl.BlockSpec((d, te), lambda i, c: (0, jnp.maximum(c - 1, 0)))],
        out_specs=tok,
        out_shape=jax.ShapeDtypeStruct((n, d), F32),
        scratch_shapes=[pltpu.VMEM(hk, F32), pltpu.VMEM(hk, BF16), pltpu.VMEM(hk, F32),
                        pltpu.VMEM(hk, BF16), pltpu.VMEM((te, tm), BF16), pltpu.VMEM((d, tm), F32),
                        pltpu.VMEM((PEER_HEADS * PEER_DKEY, tm), BF16),
                        pltpu.VMEM((PEER_TOPK, tm), F32), pltpu.VMEM((PEER_TOPK, tm), F32),
                        pltpu.VMEM((PEER_TOPK, tm), F32), pltpu.VMEM((N_KEYS, tm), F32)],
        compiler_params=_cparams("arbitrary", "arbitrary"),
        name="peer",
    )(t, y, w['wqt'], w['sk1'], w['sk2'], w['u'], w['vt'])


def _prep_weights(ln1_g, w_in, a_q_norm_g, a_kv_norm_g, a_w_uq, a_w_ukv, a_qk_g, b_qk_g,
                  w_o_a, w_o_b, w_out, ln2_g, peer_wq, peer_subkeys, peer_u, peer_v):
    d = w_in.shape[0]
    offs = np.cumsum(IN_SPLITS).tolist()
    cq, ckv, kpe, bq, bk, bv, iq, iw, ik, gates = jnp.split(w_in, offs, axis=1)
    z = lambda n: jnp.zeros((d, n), w_in.dtype)
    win = jnp.concatenate([
        cq, ckv, z(A_NOPE), kpe, z(LANES - A_NOPE - A_ROPE), bq,
        bk, z(LANES - B_HEAD_DIM), bv, z(LANES - B_HEAD_DIM), iq,
        iw, z(LANES - IDX_HEADS), ik, z(LANES - IDX_DIM)], axis=1).astype(BF16)
    assert win.shape[1] == N_SLAB_SMALL * LANES

    def pad_last(a, n):
        return jnp.pad(a, [(0, 0)] * (a.ndim - 1) + [(0, n - a.shape[-1])])

    wuq = pad_last(a_w_uq.reshape(A_QRANK, A_HEADS, A_NOPE + A_ROPE), LANES)
    wuq = wuq.reshape(A_QRANK, A_HEADS * LANES).astype(BF16)
    ukv = a_w_ukv.reshape(A_KVRANK, A_HEADS, A_NOPE + A_V)
    wk = pad_last(ukv[..., :A_NOPE], LANES).reshape(A_KVRANK, A_HEADS * LANES).astype(BF16)
    wv = pad_last(ukv[..., A_NOPE:], LANES).reshape(A_KVRANK, A_HEADS * LANES).astype(BF16)
    woa = pad_last(w_o_a.reshape(A_HEADS, A_V, d).transpose(0, 2, 1), LANES)
    woa = woa.transpose(0, 2, 1).reshape(A_HEADS * LANES, d).astype(BF16)

    row = lambda v: pad_last(v, LANES)[None, :]
    gs = jnp.concatenate([
        row(a_qk_g[0]),
        row(jnp.concatenate([jnp.zeros((A_NOPE,), F32), a_qk_g[1, A_NOPE:]])),
        row(jnp.concatenate([b_qk_g[0], b_qk_g[0]])),
        row(b_qk_g[1]),
        jnp.zeros((4, LANES), F32)], axis=0)
    return dict(
        ln1=ln1_g[None, :], win=win, wg=gates.astype(BF16), gq=a_q_norm_g[None, :],
        gkv=a_kv_norm_g[None, :], wuq=wuq, gs=gs, wk=wk, wv=wv, gk=row(a_qk_g[1, :A_NOPE]),
        woa=woa, wob=w_o_b.astype(BF16), wout=w_out.astype(BF16), ln2=ln2_g[None, :],
        wqt=peer_wq.T.astype(BF16),
        sk1=peer_subkeys[:, 0].astype(BF16), sk2=peer_subkeys[:, 1].astype(BF16),
        u=peer_u.astype(BF16), vt=peer_v.T.astype(BF16))


def _key_layout(meta, rows):
    pad = jnp.zeros((meta.shape[0], LANES - meta.shape[1], meta.shape[2]), meta.dtype)
    return jnp.concatenate([meta, pad, rows], axis=1)


def _pair_slabs(k):
    kb = k.astype(BF16)
    z = jnp.zeros_like(kb)
    return jnp.stack([jnp.concatenate([kb, z], -1), jnp.concatenate([z, kb], -1)], axis=1)


def _mixers_and_peer(x, rows, keys, qc, kc, kidx, k_sel, w, tq, tm, te, groups):
    lat_k, kpes_k, bk_k, bv_k, ik_k = keys
    (_, _, _, _, _, qa, qb, iq, iw, ga, gb) = rows
    b = lat_k.shape[0]
    ka, va = _kv_prep(lat_k, kpes_k, w)
    ik2, bk2, bv2 = _pair_slabs(ik_k), _pair_slabs(bk_k), _pair_slabs(bv_k)
    oa, ob = [], []
    for q0, nqg, s in groups:
        o = _mla_attn(qa, ka, va, qc, kc, tq, q0, nqg, s)
        oa.append(o.reshape(b, nqg * tq, o.shape[-1]))
        o = _dsa_attn(iq, iw, qb, ik2, bk2, bv2, qc, kc, kidx, tq, k_sel, q0, nqg, s)
        ob.append(o.reshape(b, nqg * tq, o.shape[-1]))
    oa = jnp.concatenate(oa, axis=1).reshape(x.shape[0], -1)
    ob = jnp.concatenate(ob, axis=1).reshape(x.shape[0], -1)
    y1, t2 = _out_proj(x, oa, ob, ga, gb, w, tm)
    tp = 2 * tm if x.shape[0] % (2 * tm) == 0 else tm
    return _peer(t2, y1, w, tp, te)


def kernel(x_prompt, x_sample, cache_a_latent, cache_a_kpe, cache_b_k, cache_b_v, cache_b_idx_k,
           meta_tokens, ln1_g, w_in, a_q_norm_g, a_kv_norm_g, a_w_uq, a_w_ukv, a_qk_g, b_qk_g,
           w_o_a, w_o_b, w_out, ln2_g, peer_wq, peer_subkeys, peer_u, peer_v):
    bp, sp, d = x_prompt.shape
    bs, ts, _ = x_sample.shape
    depth, _, past, _ = cache_a_latent.shape
    assert depth == 1 and sp % CHUNK == 0
    w = _prep_weights(ln1_g[0], w_in[0], a_q_norm_g[0], a_kv_norm_g[0], a_w_uq[0], a_w_ukv[0],
                      a_qk_g[0], b_qk_g[0], w_o_a[0], w_o_b[0], w_out[0], ln2_g[0], peer_wq[0],
                      peer_subkeys[0], peer_u[0], peer_v[0])
    tm = 256
    te = 512
    i32 = jnp.int32

    xq = x_prompt.reshape(bp * sp, d)
    rows_p = _in_proj(xq, _all_tables(N_META + jnp.arange(sp, dtype=i32)), bp, tm, w)
    rows_m = _in_proj(meta_tokens.astype(F32), _all_tables(jnp.arange(N_META, dtype=i32)), 1, N_META, w)
    xs = x_sample.reshape(bs * ts, d)
    pos_s = past + jnp.arange(ts, dtype=i32)
    rows_s = _in_proj(xs, _all_tables(jnp.tile(pos_s, bs)), 1, bs * ts, w)

    s_p = LANES + sp
    per_b = lambda a, b, t: a.reshape(b, t, a.shape[-1])
    meta_b = lambda a: jnp.broadcast_to(a[None], (bp,) + a.shape)
    keys_p = tuple(_key_layout(meta_b(m), per_b(r, bp, sp)) for r, m in zip(rows_p[:5], rows_m[:5]))
    chunk_q = jnp.arange(sp, dtype=i32) // CHUNK
    n_pad = LANES - N_META
    kc_p = jnp.concatenate([jnp.full((N_META,), -1, i32), jnp.full((n_pad,), PAD_CHUNK, i32),
                            chunk_q])[None, :]
    kidx_p = jnp.concatenate([jnp.arange(N_META, dtype=i32), s_p + jnp.arange(n_pad, dtype=i32),
                              N_META + jnp.arange(sp, dtype=i32)])[None, :]
    tq = 256
    nq = sp // tq
    tpg = max(1, nq // PROMPT_GROUPS)
    groups_p = [(q0, min(tpg, nq - q0), LANES + min(q0 + tpg, nq) * tq) for q0 in range(0, nq, tpg)]
    y_p = _mixers_and_peer(xq, rows_p, keys_p, chunk_q[:, None], kc_p, kidx_p,
                           min(IDX_TOPK, sp // 4), w, tq, tm, te, groups_p)

    s_s = _round_up(past + ts, LANES)
    kpes_cache = jnp.pad(cache_a_kpe[0], ((0, 0), (0, 0), (A_NOPE, LANES - A_NOPE - A_ROPE)))
    caches = (cache_a_latent[0], kpes_cache, cache_b_k[0], cache_b_v[0], cache_b_idx_k[0])
    keys_s = tuple(jnp.pad(jnp.concatenate([c.astype(F32), per_b(r, bs, ts)], axis=1),
                           ((0, 0), (0, s_s - past - ts), (0, 0)))
                   for c, r in zip(caches, rows_s[:5]))
    kc_s = jnp.concatenate([jnp.zeros((past,), i32), jnp.ones((ts,), i32),
                            jnp.full((s_s - past - ts,), PAD_CHUNK, i32)])[None, :]
    kidx_s = jnp.arange(s_s, dtype=i32)[None, :]
    y_s = _mixers_and_peer(xs, rows_s, keys_s, jnp.ones((ts, 1), i32), kc_s, kidx_s,
                           min(IDX_TOPK, (past + ts) // 4), w, ts, bs * ts, te, [(0, 1, s_s)])

    kpe_of = lambda r: r[:, A_NOPE:A_NOPE + A_ROPE]

    def new_rows_p(r, m):
        return jnp.concatenate([meta_b(m), per_b(r, bp, sp)], axis=1)[None]

    outs_p = [new_rows_p(rows_p[0], rows_m[0]), new_rows_p(kpe_of(rows_p[1]), kpe_of(rows_m[1]))]
    outs_p += [new_rows_p(rows_p[i], rows_m[i]) for i in (2, 3, 4)]
    outs_s = [per_b(rows_s[0], bs, ts)[None], per_b(kpe_of(rows_s[1]), bs, ts)[None]]
    outs_s += [per_b(rows_s[i], bs, ts)[None] for i in (2, 3, 4)]
    return (y_p.reshape(bp, sp, d), y_s.reshape(bs, ts, d), *outs_p, *outs_s)
```

```python
import functools

import jax
import jax.numpy as jnp
import numpy as np
from jax import lax
from jax.experimental import pallas as pl
from jax.experimental.pallas import tpu as pltpu

F32 = jnp.float32
BF16 = jnp.bfloat16

LANES = 128
BF16_ROWS = 16
VMEM_LIMIT = 52 * 1024 * 1024

CHUNK = 64
N_META = 16
ROPE_THETA = 500000.0
EPS = 1e-6

A_HEADS = 8
A_NOPE = 64
A_ROPE = 32
A_V = 64
A_QRANK = 256
A_KVRANK = 128
A_SCALE = (A_NOPE + A_ROPE) ** -0.5
MLA_HEADS_PER_STEP = 4
KV_PREP_ROWS = 1536

B_HEADS = 8
B_HEAD_DIM = 64
B_ROT = B_HEAD_DIM // 4
B_SCALE = B_HEAD_DIM ** -0.5
IDX_HEADS = 8
IDX_DIM = 64
IDX_ROT = IDX_DIM // 4
IDX_TOPK = 256
IDX_W_SCALE = (IDX_HEADS * IDX_DIM) ** -0.5

PEER_HEADS = 8
PEER_DKEY = 128
N_KEYS = 128
PEER_TOPK = 16
PEER_LINK_EVERY = 2
PROMPT_GROUPS = 16

IN_SPLITS = (A_QRANK, A_KVRANK, A_ROPE, B_HEADS * B_HEAD_DIM, B_HEAD_DIM, B_HEAD_DIM,
             IDX_HEADS * IDX_DIM, IDX_HEADS, IDX_DIM)

NEG_BIG = -1e30
LOG2_E = 1.4426950408889634
INT_MIN = -2147483648
PAD_CHUNK = 1 << 30

NT_DIMS = (((1,), (1,)), ((), ()))


def _cparams(*sem):
    return pltpu.CompilerParams(dimension_semantics=sem, vmem_limit_bytes=VMEM_LIMIT)


def _round_up(n, m):
    return -(-n // m) * m


def _pick_tile(n, target):
    best = LANES
    for t in range(LANES, min(n, target) + 1, LANES):
        if n % t == 0:
            best = t
    return best


def _rot_tables(pos, rot_dim, offsets):
    half = rot_dim // 2
    inv = ROPE_THETA ** (-jnp.arange(half, dtype=F32) * (2.0 / rot_dim))
    freq_idx = np.zeros((LANES,), np.int32)
    first = np.zeros((LANES,), bool)
    second = np.zeros((LANES,), bool)
    for o in offsets:
        freq_idx[o:o + rot_dim] = np.arange(rot_dim) % half
        first[o:o + half] = True
        second[o + half:o + rot_dim] = True
    ang = pos.astype(F32)[:, None] * inv[freq_idx][None, :]
    cos, sin = jnp.cos(ang), jnp.sin(ang)
    c = jnp.where((first | second)[None, :], cos, 1.0)
    sm = jnp.where(first[None, :], -sin, 0.0)
    sp = jnp.where(second[None, :], sin, 0.0)
    return jnp.stack([c, sm, sp])


def _all_tables(pos):
    return jnp.concatenate([
        _rot_tables(pos, A_ROPE, (A_NOPE,)),
        _rot_tables(pos, B_ROT, (0, B_HEAD_DIM)),
        _rot_tables(pos, B_ROT, (0,)),
    ])


def _rot(x, tab_ref, base, half):
    c = tab_ref[base]
    sm = tab_ref[base + 1]
    sp = tab_ref[base + 2]
    return x * c + pltpu.roll(x, LANES - half, 1) * sm + pltpu.roll(x, half, 1) * sp


SLAB_CQ, SLAB_CKV, SLAB_KPE, SLAB_BQ, SLAB_BK, SLAB_BV = 0, 2, 3, 4, 8, 9
SLAB_IQ, SLAB_IW, SLAB_IK, SLAB_GA, N_SLAB_SMALL = 10, 14, 15, 16, 16


GATE_COLS = 512


def _cols(s0, n=1):
    return slice(s0 * LANES, (s0 + n) * LANES)


def _inproj_kernel(x_ref, ln1_ref, win_ref, wg_ref, gq_ref, gkv_ref, wuq_ref, gs_ref, tab_ref,
                   lat_ref, kpes_ref, bk_ref, bv_ref, ik_ref, qa_ref, qb_ref, iq_ref, iw_ref,
                   ga_ref, gb_ref):
    d = x_ref.shape[1]
    x = x_ref[...]
    h = x * lax.rsqrt(jnp.mean(x * x, axis=-1, keepdims=True) + EPS) * ln1_ref[...]
    hb = h.astype(BF16)
    lane = lax.broadcasted_iota(jnp.int32, (1, LANES), 1)
    lo = lane < B_HEAD_DIM

    def proj(s0, n=1):
        return jnp.dot(hb, win_ref[:, _cols(s0, n)], preferred_element_type=F32)

    def seg_rs(x2, mask, n):
        return lax.rsqrt(jnp.sum(jnp.where(mask, x2, 0.0), axis=-1, keepdims=True) * (1.0 / n) + EPS)

    cq = proj(SLAB_CQ, 2)
    cq = cq * lax.rsqrt(jnp.mean(cq * cq, axis=-1, keepdims=True) + EPS) * gq_ref[...]
    q = jnp.dot(cq.astype(BF16), wuq_ref[...], preferred_element_type=F32)
    rope_m = jnp.logical_and(lane >= A_NOPE, lane < A_NOPE + A_ROPE)
    for hd in range(A_HEADS):
        s = q[:, _cols(hd)]
        s2 = s * s
        sc = jnp.where(lo, seg_rs(s2, lo, A_NOPE), seg_rs(s2, rope_m, A_ROPE))
        qa_ref[:, _cols(hd)] = _rot(s * sc * gs_ref[0:1, :], tab_ref, 0, A_ROPE // 2).astype(BF16)

    both = proj(SLAB_CKV, 2)
    ckv = both[:, _cols(0)]
    lat_ref[...] = ckv * lax.rsqrt(jnp.mean(ckv * ckv, axis=-1, keepdims=True) + EPS) * gkv_ref[...]
    kp = both[:, _cols(1)]
    kp = kp * seg_rs(kp * kp, rope_m, A_ROPE) * gs_ref[1:2, :]
    kpes_ref[...] = _rot(kp, tab_ref, 0, A_ROPE // 2)

    bq = proj(SLAB_BQ, B_HEADS // 2)
    for j in range(B_HEADS // 2):
        s = bq[:, _cols(j)]
        s2 = s * s
        sc = jnp.where(lo, seg_rs(s2, lo, B_HEAD_DIM), seg_rs(s2, jnp.logical_not(lo), B_HEAD_DIM))
        s = _rot(s * sc * gs_ref[2:3, :], tab_ref, 3, B_ROT // 2) * B_SCALE
        qb_ref[:, _cols(j)] = s.astype(BF16)
    both = proj(SLAB_BK, 2)
    s = both[:, _cols(0)]
    s = s * seg_rs(s * s, lo, B_HEAD_DIM) * gs_ref[3:4, :]
    bk_ref[...] = _rot(s, tab_ref, 6, B_ROT // 2)[:, :B_HEAD_DIM]
    bv_ref[...] = both[:, LANES:LANES + B_HEAD_DIM]

    iq = proj(SLAB_IQ, IDX_HEADS // 2)
    for j in range(IDX_HEADS // 2):
        iq_ref[:, _cols(j)] = _rot(iq[:, _cols(j)], tab_ref, 3, IDX_ROT // 2).astype(BF16)
    both = proj(SLAB_IW, 2)
    iw_ref[...] = both[:, _cols(0)] * IDX_W_SCALE
    ik_ref[...] = _rot(both[:, _cols(1)], tab_ref, 6, IDX_ROT // 2)[:, :IDX_DIM]

    for j in range(0, d, GATE_COLS):
        ga_ref[:, j:j + GATE_COLS] = jax.nn.sigmoid(
            jnp.dot(hb, wg_ref[:, j:j + GATE_COLS], preferred_element_type=F32))
        gb_ref[:, j:j + GATE_COLS] = jax.nn.sigmoid(
            jnp.dot(hb, wg_ref[:, d + j:d + j + GATE_COLS], preferred_element_type=F32))


def _in_proj(x, tabs, reps, tm, w):
    n, d = x.shape
    r = tabs.shape[1]
    npb = r // tm
    assert r % tm == 0 and n == reps * r
    tok = lambda width: pl.BlockSpec((tm, width), lambda p, b: (b * npb + p, 0))
    full = lambda a: pl.BlockSpec(a.shape, lambda p, b: (0,) * a.ndim)
    outs = [
        ((n, A_KVRANK), F32), ((n, LANES), F32), ((n, B_HEAD_DIM), F32), ((n, B_HEAD_DIM), F32),
        ((n, IDX_DIM), F32), ((n, A_HEADS * LANES), BF16), ((n, B_HEADS * B_HEAD_DIM), BF16),
        ((n, IDX_HEADS * IDX_DIM), BF16), ((n, LANES), F32), ((n, d), F32), ((n, d), F32),
    ]
    return pl.pallas_call(
        _inproj_kernel,
        grid=(npb, reps),
        in_specs=[tok(d), full(w['ln1']), full(w['win']), full(w['wg']), full(w['gq']), full(w['gkv']),
                  full(w['wuq']), full(w['gs']),
                  pl.BlockSpec((tabs.shape[0], tm, LANES), lambda p, b: (0, p, 0))],
        out_specs=[tok(s[1]) for s, _ in outs],
        out_shape=[jax.ShapeDtypeStruct(s, t) for s, t in outs],
        compiler_params=_cparams("arbitrary", "arbitrary"),
        name="in_proj",
    )(x, w['ln1'], w['win'], w['wg'], w['gq'], w['gkv'], w['wuq'], w['gs'], tabs)


def _kvprep_kernel(lat_ref, kpes_ref, wk_ref, wv_ref, gk_ref, ka_ref, va_ref):
    lat = lat_ref[...].astype(BF16)
    kk = jnp.dot(lat, wk_ref[...], preferred_element_type=F32)
    vv = jnp.dot(lat, wv_ref[...], preferred_element_type=F32)
    kpes = kpes_ref[...]
    for hd in range(A_HEADS):
        s = kk[:, _cols(hd)]
        rs = lax.rsqrt(jnp.sum(s * s, axis=-1, keepdims=True) * (1.0 / A_NOPE) + EPS)
        ka_ref[hd] = (s * rs * gk_ref[...] + kpes).astype(BF16)
        va_ref[hd] = vv[:, _cols(hd)].astype(BF16)


def _kv_prep(lat_keys, kpes_keys, w):
    b, s, _ = lat_keys.shape
    ts = _pick_tile(s, KV_PREP_ROWS)
    row = pl.BlockSpec((None, ts, LANES), lambda i, j: (i, j, 0))
    full = lambda a: pl.BlockSpec(a.shape, lambda i, j: (0,) * a.ndim)
    hd = pl.BlockSpec((None, A_HEADS, ts, LANES), lambda i, j: (i, 0, j, 0))
    shp = jax.ShapeDtypeStruct((b, A_HEADS, s, LANES), BF16)
    return pl.pallas_call(
        _kvprep_kernel,
        grid=(b, s // ts),
        in_specs=[row, row, full(w['wk']), full(w['wv']), full(w['gk'])],
        out_specs=[hd, hd],
        out_shape=[shp, shp],
        compiler_params=_cparams("arbitrary", "arbitrary"),
        name="kv_prep",
    )(lat_keys, kpes_keys, w['wk'], w['wv'], w['gk'])


def _mla_kernel(q_ref, k_ref, v_ref, qc_ref, kc_ref, o_ref, bias_s):
    j = pl.program_id(2)

    @pl.when(pl.program_id(1) == 0)
    def _():
        bias_s[j] = jnp.where(kc_ref[...] <= qc_ref[...], 0.0, NEG_BIG)

    for hd in range(MLA_HEADS_PER_STEP):
        s = lax.dot_general(q_ref[:, _cols(hd)], k_ref[hd], NT_DIMS, preferred_element_type=F32)
        s = s + bias_s[j]
        p = jnp.exp2((s - jnp.max(s, axis=-1, keepdims=True)) * (A_SCALE * LOG2_E))
        inv_l = 1.0 / jnp.sum(p, axis=-1, keepdims=True)
        o = jnp.dot(p.astype(BF16), v_ref[hd], preferred_element_type=F32)
        o_ref[:, _cols(hd)] = (o * inv_l).astype(BF16)


def _mla_attn(qa, ka, va, qc, kc, tq, q0, nqg, s):
    n = qa.shape[0]
    b = ka.shape[0]
    nq = n // (b * tq)
    hps = MLA_HEADS_PER_STEP
    qspec = pl.BlockSpec((tq, hps * LANES), lambda i, h, j: (i * nq + q0 + j, h))
    kspec = pl.BlockSpec((None, hps, s, LANES), lambda i, h, j: (i, h, 0, 0))
    return pl.pallas_call(
        _mla_kernel,
        grid=(b, A_HEADS // hps, nqg),
        in_specs=[qspec, kspec, kspec,
                  pl.BlockSpec((tq, 1), lambda i, h, j: (q0 + j, 0)),
                  pl.BlockSpec((1, s), lambda i, h, j: (0, 0))],
        out_specs=pl.BlockSpec((tq, hps * LANES), lambda i, h, j: (i * nqg + j, h)),
        out_shape=jax.ShapeDtypeStruct((b * nqg * tq, A_HEADS * LANES), BF16),
        scratch_shapes=[pltpu.VMEM((nqg, tq, s), F32)],
        compiler_params=_cparams("arbitrary", "arbitrary", "arbitrary"),
        name="mla_attn",
    )(qa, ka, va, qc, kc)


def _dsa_kernel(iq_ref, iw_ref, qb_ref, ik2_ref, bk2_ref, bv2_ref, qc_ref, kc_ref, kidx_ref,
                o_ref, key_s, bias_s, *, k_sel, idx_bits):
    tq = key_s.shape[0]
    adm = kc_ref[...] <= qc_ref[...]
    iw = iw_ref[...]
    score = None
    for j in range(IDX_HEADS // 2):
        iqj = iq_ref[:, _cols(j)]
        for half in range(2):
            hd = 2 * j + half
            rel = jnp.maximum(
                lax.dot_general(iqj, ik2_ref[half], NT_DIMS, preferred_element_type=F32), 0.0)
            term = iw[:, hd:hd + 1] * rel
            score = term if score is None else score + term

    bits = lax.bitcast_convert_type(score, jnp.int32)
    key = jnp.where(bits < 0, bits ^ jnp.int32(0x7FFFFFFF), bits)
    key = jnp.where(key == -1, 0, key)
    key_s[...] = jnp.where(adm, key, INT_MIN)

    kf = float(k_sel)

    def count(mask):
        return jnp.sum(jnp.where(mask, 1.0, 0.0), axis=-1, keepdims=True)

    def value_step(i, cur):
        cand = cur + lax.shift_left(jnp.int32(1), 31 - i)
        return jnp.where(count(key_s[...] >= cand) >= kf, cand, cur)

    tau = lax.fori_loop(0, 32, value_step, jnp.full((tq, 1), INT_MIN, jnp.int32))
    key = key_s[...]
    ge = key >= tau
    c_ge = count(ge)
    bias_s[...] = jnp.where(jnp.logical_and(ge, adm), 0.0, NEG_BIG)
    amb = jnp.logical_and(tau > INT_MIN, c_ge > kf)

    @pl.when(jnp.max(jnp.where(amb, 1.0, 0.0)) > 0.0)
    def _():
        key = key_s[...]
        gt = key > tau
        tie = key == tau
        need = kf - count(gt)
        kidx = kidx_ref[...]

        def index_step(i, cur):
            cand = cur + lax.shift_left(jnp.int32(1), idx_bits - 1 - i)
            below = count(jnp.logical_and(tie, kidx < cand))
            return jnp.where(below < need, cand, cur)

        last = lax.fori_loop(0, idx_bits, index_step, jnp.zeros((tq, 1), jnp.int32))
        sel = jnp.logical_or(gt, jnp.logical_and(tie, kidx <= last))
        bias_s[...] = jnp.where(jnp.logical_and(sel, adm), 0.0, NEG_BIG)

    for j in range(B_HEADS // 2):
        qbj = qb_ref[:, _cols(j)]
        acc = None
        for half in range(2):
            lg = lax.dot_general(qbj, bk2_ref[half], NT_DIMS, preferred_element_type=F32)
            lg = lg + bias_s[...]
            p = jnp.exp(lg - jnp.max(lg, axis=-1, keepdims=True))
            inv_l = 1.0 / jnp.sum(p, axis=-1, keepdims=True)
            o = jnp.dot(p.astype(BF16), bv2_ref[half], preferred_element_type=F32) * inv_l
            acc = o if acc is None else acc + o
        o_ref[:, _cols(j)] = acc.astype(BF16)


def _dsa_attn(iq, iw, qb, ik2, bk2, bv2, qc, kc, kidx, tq, k_sel, q0, nqg, s):
    n = iq.shape[0]
    b, _, s_all, _ = ik2.shape
    nq = n // (b * tq)
    idx_bits = max(1, int(np.ceil(np.log2(s_all + N_META + 1))))
    tok = lambda width: pl.BlockSpec((tq, width), lambda i, j: (i * nq + q0 + j, 0))
    kspec = pl.BlockSpec((None, 2, s, LANES), lambda i, j: (i, 0, 0, 0))
    row = pl.BlockSpec((1, s), lambda i, j: (0, 0))
    return pl.pallas_call(
        functools.partial(_dsa_kernel, k_sel=k_sel, idx_bits=idx_bits),
        grid=(b, nqg),
        in_specs=[tok(iq.shape[1]), tok(LANES), tok(qb.shape[1]), kspec, kspec, kspec,
                  pl.BlockSpec((tq, 1), lambda i, j: (q0 + j, 0)), row, row],
        out_specs=pl.BlockSpec((tq, qb.shape[1]), lambda i, j: (i * nqg + j, 0)),
        out_shape=jax.ShapeDtypeStruct((b * nqg * tq, qb.shape[1]), BF16),
        scratch_shapes=[pltpu.VMEM((tq, s), jnp.int32), pltpu.VMEM((tq, s), F32)],
        compiler_params=_cparams("arbitrary", "arbitrary"),
        name="dsa_attn",
    )(iq, iw, qb, ik2, bk2, bv2, qc, kc, kidx)


def _outproj_kernel(x_ref, oa_ref, ob_ref, ga_ref, gb_ref, woa_ref, wob_ref, wout_ref, ln2_ref,
                    y_ref, t_ref):
    ya = jnp.dot(oa_ref[...], woa_ref[...], preferred_element_type=F32)
    yb = jnp.dot(ob_ref[...], wob_ref[...], preferred_element_type=F32)
    mg = ga_ref[...] * ya + gb_ref[...] * yb
    y = x_ref[...] + jnp.dot(mg.astype(BF16), wout_ref[...], preferred_element_type=F32)
    y_ref[...] = y
    t = y * lax.rsqrt(jnp.mean(y * y, axis=-1, keepdims=True) + EPS) * ln2_ref[...]
    t_ref[...] = t.astype(BF16)


def _out_proj(x, oa, ob, ga, gb, w, tm):
    n, d = x.shape
    tok = lambda a: pl.BlockSpec((tm, a.shape[1]), lambda i: (i, 0))
    full = lambda a: pl.BlockSpec(a.shape, lambda i: (0,) * a.ndim)
    return pl.pallas_call(
        _outproj_kernel,
        grid=(n // tm,),
        in_specs=[tok(x), tok(oa), tok(ob), tok(ga), tok(gb),
                  full(w['woa']), full(w['wob']), full(w['wout']), full(w['ln2'])],
        out_specs=[tok(x), tok(x)],
        out_shape=[jax.ShapeDtypeStruct((n, d), F32), jax.ShapeDtypeStruct((n, d), BF16)],
        compiler_params=_cparams("arbitrary"),
        name="out_proj",
    )(x, oa, ob, ga, gb, w['woa'], w['wob'], w['wout'], w['ln2'])


def _top_rows(s, k, with_rank=False):
    t = s.shape[1]
    row = lax.broadcasted_iota(jnp.int32, (k, t), 0).astype(F32)
    out = jnp.zeros((k, t), F32)
    seen = jnp.zeros((1, t), F32)
    rank = jnp.full(s.shape, float(k), F32)
    for _ in range(k):
        m = jnp.max(s, axis=0, keepdims=True)
        eq = s == m
        if with_rank:
            rank = jnp.where(eq, seen, rank)
        upto = seen + jnp.sum(jnp.where(eq, 1.0, 0.0), axis=0, keepdims=True)
        out = jnp.where(jnp.logical_and(row >= seen, row < upto), m, out)
        seen = upto
        s = jnp.where(eq, -jnp.inf, s)
    return (out, rank) if with_rank else out


def _top_rows_distinct(s, k, with_rank):
    t = s.shape[1]
    row = lax.broadcasted_iota(jnp.int32, (k, t), 0)
    out = jnp.zeros((k, t), F32)
    rank = jnp.full(s.shape, float(k), F32)
    for q in range(k):
        m = jnp.max(s, axis=0, keepdims=True)
        eq = s == m
        if with_rank:
            rank = jnp.where(eq, float(q), rank)
        out = jnp.where(row == q, m, out)
        s = jnp.where(eq, -jnp.inf, s)
    used = jnp.sum(jnp.where(s == -jnp.inf, 1.0, 0.0), axis=0, keepdims=True)
    return out, rank, used


def _top_rows_into(s, k, n_inf, out_s, rank_s=None):
    out, rank, used = _top_rows_distinct(s, k, rank_s is not None)
    out_s[...] = out
    if rank_s is not None:
        rank_s[...] = rank

    @pl.when(jnp.max(used) > float(k + n_inf))
    def _():
        if rank_s is None:
            out_s[...] = _top_rows(s, k)
        else:
            out_s[...], rank_s[...] = _top_rows(s, k, with_rank=True)


def _pair_sum_candidates(a, b):
    k = PEER_TOPK
    assert k == 16 and a.shape[0] == k
    row = lax.broadcasted_iota(jnp.int32, (8, a.shape[1]), 0)
    blocks = [a[0:1] + b]
    n_inf = 0
    for i in range(1, 8):
        blocks.append(jnp.where(row < k // (i + 1), a[i:i + 1] + b[:8], -jnp.inf))
        n_inf += 8 - k // (i + 1)
    blocks.append(a[8:] + b[0:1])
    return jnp.concatenate(blocks, axis=0), n_inf


def _peer_kernel(t_ref, y_ref, wqt_ref, sk1_ref, sk2_ref, u_ref, vt_ref, o_ref,
                 m_s, r_s, a_s, b_s, gw_s, acc_s, qt_s, t1_s, t2_s, tp_s, rk_s):
    c = pl.program_id(1)
    nc = pl.num_programs(1) - 1
    te = u_ref.shape[0]
    half = PEER_DKEY // 2

    @pl.when(c == 0)
    def _route():
        qt = lax.dot_general(wqt_ref[...], t_ref[...], NT_DIMS, preferred_element_type=F32)
        qt_s[...] = qt.astype(BF16)

        @pl.loop(0, PEER_HEADS)
        def _head(hd):
            row0 = pl.multiple_of(hd * PEER_DKEY, PEER_DKEY)
            q1 = qt_s[pl.ds(row0, half), :]
            q2 = qt_s[pl.ds(row0 + half, half), :]
            s1 = jnp.dot(sk1_ref[hd], q1, preferred_element_type=F32)
            s2 = jnp.dot(sk2_ref[hd], q2, preferred_element_type=F32)
            _top_rows_into(s1, PEER_TOPK, 0, t1_s)
            _top_rows_into(s2, PEER_TOPK, 0, t2_s, rk_s)
            top1, top2, r = t1_s[...], t2_s[...], rk_s[...]
            cand, n_inf = _pair_sum_candidates(top1, top2)
            _top_rows_into(cand, PEER_TOPK, n_inf, tp_s)
            top = tp_s[...]
            tau = top[PEER_TOPK - 1:PEER_TOPK]
            z = jnp.sum(jnp.exp(top - top[0:1]), axis=0, keepdims=True)
            m = jnp.zeros(s1.shape, F32)
            for j in range(PEER_TOPK):
                reach = top1 + top2[j:j + 1] >= tau
                c_j = jnp.min(jnp.where(reach, top1, jnp.inf), axis=0, keepdims=True)
                m = m + jnp.where(s1 >= c_j, 1.0, 0.0)
            m_s[hd] = m
            r_s[hd] = r.astype(BF16)
            a_s[hd] = jnp.exp(s1 - top1[0:1])
            b_s[hd] = (jnp.exp(s2 - top2[0:1]) * (1.0 / z)).astype(BF16)
        gw_s[...] = jnp.zeros_like(gw_s)
        acc_s[...] = jnp.zeros_like(acc_s)

    acc_s[...] += jnp.dot(vt_ref[...], gw_s[...], preferred_element_type=F32)
    at = lax.dot_general(u_ref[...], t_ref[...], NT_DIMS, preferred_element_type=F32)
    zero = jnp.zeros((), BF16)
    tm = t_ref.shape[0]
    pk = (N_KEYS // BF16_ROWS, BF16_ROWS, tm)
    lw = min(LANES, tm)
    tok = jnp.zeros((BF16_ROWS, lw), F32)
    for k in range(te // N_KEYS):
        i1 = jnp.minimum(c, nc - 1) * (te // N_KEYS) + k
        w = None
        for hd in range(PEER_HEADS):
            hold = jnp.tile(tok * 0.0, (1, tm // lw))
            m_row = (jnp.broadcast_to(m_s[hd, pl.ds(i1, 1), :], pk[1:]) + hold).astype(BF16)
            a_row = jnp.broadcast_to(a_s[hd, pl.ds(i1, 1), :], pk[1:]).astype(BF16)
            sel = r_s[hd].reshape(pk) < m_row[None]
            term = jnp.where(sel, b_s[hd].reshape(pk) * a_row[None], zero)
            tok = tok + term[0, :, :lw].astype(F32)
            if (k * PEER_HEADS + hd) % PEER_LINK_EVERY == PEER_LINK_EVERY - 1:
                tok = pltpu.roll(tok, 1, 1)
            w = term if w is None else w + term
        g = jax.nn.gelu(at[k * N_KEYS:(k + 1) * N_KEYS])
        gw_s[k * N_KEYS:(k + 1) * N_KEYS, :] = w.reshape(N_KEYS, tm) * g.astype(BF16)

    @pl.when(c == nc)
    def _finish():
        o_ref[...] = y_ref[...] + acc_s[...].T


def _peer(t, y, w, tm, te):
    n, d = y.shape
    nc = w['u'].shape[0] // te
    full = lambda a: pl.BlockSpec(a.shape, lambda i, c: (0,) * a.ndim)
    tok = pl.BlockSpec((tm, d), lambda i, c: (i, 0))
    hk = (PEER_HEADS, N_KEYS, tm)
    return pl.pallas_call(
        _peer_kernel,
        grid=(n // tm, nc + 1),
        in_specs=[tok, tok, full(w['wqt']), full(w['sk1']), full(w['sk2']),
                  pl.BlockSpec((te, d), lambda i, c: (jnp.minimum(c, nc - 1), 0)),
                  pl.BlockSpec((d, te), lambda i, c: (0, jnp.maximum(c - 1, 0)))],
        out_specs=tok,
        out_shape=jax.ShapeDtypeStruct((n, d), F32),
        scratch_shapes=[pltpu.VMEM(hk, F32), pltpu.VMEM(hk, BF16), pltpu.VMEM(hk, F32),
                        pltpu.VMEM(hk, BF16), pltpu.VMEM((te, tm), BF16), pltpu.VMEM((d, tm), F32),
                        pltpu.VMEM((PEER_HEADS * PEER_DKEY, tm), BF16),
                        pltpu.VMEM((PEER_TOPK, tm), F32), pltpu.VMEM((PEER_TOPK, tm), F32),
                        pltpu.VMEM((PEER_TOPK, tm), F32), pltpu.VMEM((N_KEYS, tm), F32)],
        compiler_params=_cparams("arbitrary", "arbitrary"),
        name="peer",
    )(t, y, w['wqt'], w['sk1'], w['sk2'], w['u'], w['vt'])


def _prep_weights(ln1_g, w_in, a_q_norm_g, a_kv_norm_g, a_w_uq, a_w_ukv, a_qk_g, b_qk_g,
                  w_o_a, w_o_b, w_out, ln2_g, peer_wq, peer_subkeys, peer_u, peer_v):
    d = w_in.shape[0]
    offs = np.cumsum(IN_SPLITS).tolist()
    cq, ckv, kpe, bq, bk, bv, iq, iw, ik, gates = jnp.split(w_in, offs, axis=1)
    z = lambda n: jnp.zeros((d, n), w_in.dtype)
    win = jnp.concatenate([
        cq, ckv, z(A_NOPE), kpe, z(LANES - A_NOPE - A_ROPE), bq,
        bk, z(LANES - B_HEAD_DIM), bv, z(LANES - B_HEAD_DIM), iq,
        iw, z(LANES - IDX_HEADS), ik, z(LANES - IDX_DIM)], axis=1).astype(BF16)
    assert win.shape[1] == N_SLAB_SMALL * LANES

    def pad_last(a, n):
        return jnp.pad(a, [(0, 0)] * (a.ndim - 1) + [(0, n - a.shape[-1])])

    wuq = pad_last(a_w_uq.reshape(A_QRANK, A_HEADS, A_NOPE + A_ROPE), LANES)
    wuq = wuq.reshape(A_QRANK, A_HEADS * LANES).astype(BF16)
    ukv = a_w_ukv.reshape(A_KVRANK, A_HEADS, A_NOPE + A_V)
    wk = pad_last(ukv[..., :A_NOPE], LANES).reshape(A_KVRANK, A_HEADS * LANES).astype(BF16)
    wv = pad_last(ukv[..., A_NOPE:], LANES).reshape(A_KVRANK, A_HEADS * LANES).astype(BF16)
    woa = pad_last(w_o_a.reshape(A_HEADS, A_V, d).transpose(0, 2, 1), LANES)
    woa = woa.transpose(0, 2, 1).reshape(A_HEADS * LANES, d).astype(BF16)

    row = lambda v: pad_last(v, LANES)[None, :]
    gs = jnp.concatenate([
        row(a_qk_g[0]),
        row(jnp.concatenate([jnp.zeros((A_NOPE,), F32), a_qk_g[1, A_NOPE:]])),
        row(jnp.concatenate([b_qk_g[0], b_qk_g[0]])),
        row(b_qk_g[1]),
        jnp.zeros((4, LANES), F32)], axis=0)
    return dict(
        ln1=ln1_g[None, :], win=win, wg=gates.astype(BF16), gq=a_q_norm_g[None, :],
        gkv=a_kv_norm_g[None, :], wuq=wuq, gs=gs, wk=wk, wv=wv, gk=row(a_qk_g[1, :A_NOPE]),
        woa=woa, wob=w_o_b.astype(BF16), wout=w_out.astype(BF16), ln2=ln2_g[None, :],
        wqt=peer_wq.T.astype(BF16),
        sk1=peer_subkeys[:, 0].astype(BF16), sk2=peer_subkeys[:, 1].astype(BF16),
        u=peer_u.astype(BF16), vt=peer_v.T.astype(BF16))


def _key_layout(meta, rows):
    pad = jnp.zeros((meta.shape[0], LANES - meta.shape[1], meta.shape[2]), meta.dtype)
    return jnp.concatenate([meta, pad, rows], axis=1)


def _pair_slabs(k):
    kb = k.astype(BF16)
    z = jnp.zeros_like(kb)
    return jnp.stack([jnp.concatenate([kb, z], -1), jnp.concatenate([z, kb], -1)], axis=1)


def _mixers_and_peer(x, rows, keys, qc, kc, kidx, k_sel, w, tq, tm, te, groups):
    lat_k, kpes_k, bk_k, bv_k, ik_k = keys
    (_, _, _, _, _, qa, qb, iq, iw, ga, gb) = rows
    b = lat_k.shape[0]
    ka, va = _kv_prep(lat_k, kpes_k, w)
    ik2, bk2, bv2 = _pair_slabs(ik_k), _pair_slabs(bk_k), _pair_slabs(bv_k)
    oa, ob = [], []
    for q0, nqg, s in groups:
        o = _mla_attn(qa, ka, va, qc, kc, tq, q0, nqg, s)
        oa.append(o.reshape(b, nqg * tq, o.shape[-1]))
        o = _dsa_attn(iq, iw, qb, ik2, bk2, bv2, qc, kc, kidx, tq, k_sel, q0, nqg, s)
        ob.append(o.reshape(b, nqg * tq, o.shape[-1]))
    oa = jnp.concatenate(oa, axis=1).reshape(x.shape[0], -1)
    ob = jnp.concatenate(ob, axis=1).reshape(x.shape[0], -1)
    y1, t2 = _out_proj(x, oa, ob, ga, gb, w, tm)
    tp = 2 * tm if x.shape[0] % (2 * tm) == 0 else tm
    return _peer(t2, y1, w, tp, te)


def kernel(x_prompt, x_sample, cache_a_latent, cache_a_kpe, cache_b_k, cache_b_v, cache_b_idx_k,
           meta_tokens, ln1_g, w_in, a_q_norm_g, a_kv_norm_g, a_w_uq, a_w_ukv, a_qk_g, b_qk_g,
           w_o_a, w_o_b, w_out, ln2_g, peer_wq, peer_subkeys, peer_u, peer_v):
    bp, sp, d = x_prompt.shape
    bs, ts, _ = x_sample.shape
    depth, _, past, _ = cache_a_latent.shape
    assert depth == 1 and sp % CHUNK == 0
    w = _prep_weights(ln1_g[0], w_in[0], a_q_norm_g[0], a_kv_norm_g[0], a_w_uq[0], a_w_ukv[0],
                      a_qk_g[0], b_qk_g[0], w_o_a[0], w_o_b[0], w_out[0], ln2_g[0], peer_wq[0],
                      peer_subkeys[0], peer_u[0], peer_v[0])
    tm = 256
    te = 512
    i32 = jnp.int32

    xq = x_prompt.reshape(bp * sp, d)
    rows_p = _in_proj(xq, _all_tables(N_META + jnp.arange(sp, dtype=i32)), bp, tm, w)
    rows_m = _in_proj(meta_tokens.astype(F32), _all_tables(jnp.arange(N_META, dtype=i32)), 1, N_META, w)
    xs = x_sample.reshape(bs * ts, d)
    pos_s = past + jnp.arange(ts, dtype=i32)
    rows_s = _in_proj(xs, _all_tables(jnp.tile(pos_s, bs)), 1, bs * ts, w)

    s_p = LANES + sp
    per_b = lambda a, b, t: a.reshape(b, t, a.shape[-1])
    meta_b = lambda a: jnp.broadcast_to(a[None], (bp,) + a.shape)
    keys_p = tuple(_key_layout(meta_b(m), per_b(r, bp, sp)) for r, m in zip(rows_p[:5], rows_m[:5]))
    chunk_q = jnp.arange(sp, dtype=i32) // CHUNK
    n_pad = LANES - N_META
    kc_p = jnp.concatenate([jnp.full((N_META,), -1, i32), jnp.full((n_pad,), PAD_CHUNK, i32),
                            chunk_q])[None, :]
    kidx_p = jnp.concatenate([jnp.arange(N_META, dtype=i32), s_p + jnp.arange(n_pad, dtype=i32),
                              N_META + jnp.arange(sp, dtype=i32)])[None, :]
    tq = 256
    nq = sp // tq
    tpg = max(1, nq // PROMPT_GROUPS)
    groups_p = [(q0, min(tpg, nq - q0), LANES + min(q0 + tpg, nq) * tq) for q0 in range(0, nq, tpg)]
    y_p = _mixers_and_peer(xq, rows_p, keys_p, chunk_q[:, None], kc_p, kidx_p,
                           min(IDX_TOPK, sp // 4), w, tq, tm, te, groups_p)

    s_s = _round_up(past + ts, LANES)
    kpes_cache = jnp.pad(cache_a_kpe[0], ((0, 0), (0, 0), (A_NOPE, LANES - A_NOPE - A_ROPE)))
    caches = (cache_a_latent[0], kpes_cache, cache_b_k[0], cache_b_v[0], cache_b_idx_k[0])
    keys_s = tuple(jnp.pad(jnp.concatenate([c.astype(F32), per_b(r, bs, ts)], axis=1),
                           ((0, 0), (0, s_s - past - ts), (0, 0)))
                   for c, r in zip(caches, rows_s[:5]))
    kc_s = jnp.concatenate([jnp.zeros((past,), i32), jnp.ones((ts,), i32),
                            jnp.full((s_s - past - ts,), PAD_CHUNK, i32)])[None, :]
    kidx_s = jnp.arange(s_s, dtype=i32)[None, :]
    y_s = _mixers_and_peer(xs, rows_s, keys_s, jnp.ones((ts, 1), i32), kc_s, kidx_s,
                           min(IDX_TOPK, (past + ts) // 4), w, ts, bs * ts, te, [(0, 1, s_s)])

    kpe_of = lambda r: r[:, A_NOPE:A_NOPE + A_ROPE]

    def new_rows_p(r, m):
        return jnp.concatenate([meta_b(m), per_b(r, bp, sp)], axis=1)[None]

    outs_p = [new_rows_p(rows_p[0], rows_m[0]), new_rows_p(kpe_of(rows_p[1]), kpe_of(rows_m[1]))]
    outs_p += [new_rows_p(rows_p[i], rows_m[i]) for i in (2, 3, 4)]
    outs_s = [per_b(rows_s[0], bs, ts)[None], per_b(kpe_of(rows_s[1]), bs, ts)[None]]
    outs_s += [per_b(rows_s[i], bs, ts)[None] for i in (2, 3, 4)]
    return (y_p.reshape(bp, sp, d), y_s.reshape(bs, ts, d), *outs_p, *outs_s)
```

```python
import functools

import jax
import jax.numpy as jnp
import numpy as np
from jax import lax
from jax.experimental import pallas as pl
from jax.experimental.pallas import tpu as pltpu

F32 = jnp.float32
BF16 = jnp.bfloat16

LANES = 128
BF16_ROWS = 16
VMEM_LIMIT = 52 * 1024 * 1024

CHUNK = 64
N_META = 16
ROPE_THETA = 500000.0
EPS = 1e-6

A_HEADS = 8
A_NOPE = 64
A_ROPE = 32
A_V = 64
A_QRANK = 256
A_KVRANK = 128
A_SCALE = (A_NOPE + A_ROPE) ** -0.5
MLA_HEADS_PER_STEP = 4
KV_PREP_ROWS = 1536

B_HEADS = 8
B_HEAD_DIM = 64
B_ROT = B_HEAD_DIM // 4
B_SCALE = B_HEAD_DIM ** -0.5
IDX_HEADS = 8
IDX_DIM = 64
IDX_ROT = IDX_DIM // 4
IDX_TOPK = 256
IDX_W_SCALE = (IDX_HEADS * IDX_DIM) ** -0.5

PEER_HEADS = 8
PEER_DKEY = 128
N_KEYS = 128
PEER_TOPK = 16
PEER_LINK_EVERY = 2
PROMPT_GROUPS = 16

IN_SPLITS = (A_QRANK, A_KVRANK, A_ROPE, B_HEADS * B_HEAD_DIM, B_HEAD_DIM, B_HEAD_DIM,
             IDX_HEADS * IDX_DIM, IDX_HEADS, IDX_DIM)

NEG_BIG = -1e30
LOG2_E = 1.4426950408889634
INT_MIN = -2147483648
PAD_CHUNK = 1 << 30

NT_DIMS = (((1,), (1,)), ((), ()))


def _cparams(*sem):
    return pltpu.CompilerParams(dimension_semantics=sem, vmem_limit_bytes=VMEM_LIMIT)


def _round_up(n, m):
    return -(-n // m) * m


def _pick_tile(n, target):
    best = LANES
    for t in range(LANES, min(n, target) + 1, LANES):
        if n % t == 0:
            best = t
    return best


def _rot_tables(pos, rot_dim, offsets):
    half = rot_dim // 2
    inv = ROPE_THETA ** (-jnp.arange(half, dtype=F32) * (2.0 / rot_dim))
    freq_idx = np.zeros((LANES,), np.int32)
    first = np.zeros((LANES,), bool)
    second = np.zeros((LANES,), bool)
    for o in offsets:
        freq_idx[o:o + rot_dim] = np.arange(rot_dim) % half
        first[o:o + half] = True
        second[o + half:o + rot_dim] = True
    ang = pos.astype(F32)[:, None] * inv[freq_idx][None, :]
    cos, sin = jnp.cos(ang), jnp.sin(ang)
    c = jnp.where((first | second)[None, :], cos, 1.0)
    sm = jnp.where(first[None, :], -sin, 0.0)
    sp = jnp.where(second[None, :], sin, 0.0)
    return jnp.stack([c, sm, sp])


def _all_tables(pos):
    return jnp.concatenate([
        _rot_tables(pos, A_ROPE, (A_NOPE,)),
        _rot_tables(pos, B_ROT, (0, B_HEAD_DIM)),
        _rot_tables(pos, B_ROT, (0,)),
    ])


def _rot(x, tab_ref, base, half):
    c = tab_ref[base]
    sm = tab_ref[base + 1]
    sp = tab_ref[base + 2]
    return x * c + pltpu.roll(x, LANES - half, 1) * sm + pltpu.roll(x, half, 1) * sp


SLAB_CQ, SLAB_CKV, SLAB_KPE, SLAB_BQ, SLAB_BK, SLAB_BV = 0, 2, 3, 4, 8, 9
SLAB_IQ, SLAB_IW, SLAB_IK, SLAB_GA, N_SLAB_SMALL = 10, 14, 15, 16, 16


GATE_COLS = 512


def _cols(s0, n=1):
    return slice(s0 * LANES, (s0 + n) * LANES)


def _inproj_kernel(x_ref, ln1_ref, win_ref, wg_ref, gq_ref, gkv_ref, wuq_ref, gs_ref, tab_ref,
                   lat_ref, kpes_ref, bk_ref, bv_ref, ik_ref, qa_ref, qb_ref, iq_ref, iw_ref,
                   ga_ref, gb_ref):
    d = x_ref.shape[1]
    x = x_ref[...]
    h = x * lax.rsqrt(jnp.mean(x * x, axis=-1, keepdims=True) + EPS) * ln1_ref[...]
    hb = h.astype(BF16)
    lane = lax.broadcasted_iota(jnp.int32, (1, LANES), 1)
    lo = lane < B_HEAD_DIM

    def proj(s0, n=1):
        return jnp.dot(hb, win_ref[:, _cols(s0, n)], preferred_element_type=F32)

    def seg_rs(x2, mask, n):
        return lax.rsqrt(jnp.sum(jnp.where(mask, x2, 0.0), axis=-1, keepdims=True) * (1.0 / n) + EPS)

    cq = proj(SLAB_CQ, 2)
    cq = cq * lax.rsqrt(jnp.mean(cq * cq, axis=-1, keepdims=True) + EPS) * gq_ref[...]
    q = jnp.dot(cq.astype(BF16), wuq_ref[...], preferred_element_type=F32)
    rope_m = jnp.logical_and(lane >= A_NOPE, lane < A_NOPE + A_ROPE)
    for hd in range(A_HEADS):
        s = q[:, _cols(hd)]
        s2 = s * s
        sc = jnp.where(lo, seg_rs(s2, lo, A_NOPE), seg_rs(s2, rope_m, A_ROPE))
        qa_ref[:, _cols(hd)] = _rot(s * sc * gs_ref[0:1, :], tab_ref, 0, A_ROPE // 2).astype(BF16)

    both = proj(SLAB_CKV, 2)
    ckv = both[:, _cols(0)]
    lat_ref[...] = ckv * lax.rsqrt(jnp.mean(ckv * ckv, axis=-1, keepdims=True) + EPS) * gkv_ref[...]
    kp = both[:, _cols(1)]
    kp = kp * seg_rs(kp * kp, rope_m, A_ROPE) * gs_ref[1:2, :]
    kpes_ref[...] = _rot(kp, tab_ref, 0, A_ROPE // 2)

    bq = proj(SLAB_BQ, B_HEADS // 2)
    for j in range(B_HEADS // 2):
        s = bq[:, _cols(j)]
        s2 = s * s
        sc = jnp.where(lo, seg_rs(s2, lo, B_HEAD_DIM), seg_rs(s2, jnp.logical_not(lo), B_HEAD_DIM))
        s = _rot(s * sc * gs_ref[2:3, :], tab_ref, 3, B_ROT // 2) * B_SCALE
        qb_ref[:, _cols(j)] = s.astype(BF16)
    both = proj(SLAB_BK, 2)
    s = both[:, _cols(0)]
    s = s * seg_rs(s * s, lo, B_HEAD_DIM) * gs_ref[3:4, :]
    bk_ref[...] = _rot(s, tab_ref, 6, B_ROT // 2)[:, :B_HEAD_DIM]
    bv_ref[...] = both[:, LANES:LANES + B_HEAD_DIM]

    iq = proj(SLAB_IQ, IDX_HEADS // 2)
    for j in range(IDX_HEADS // 2):
        iq_ref[:, _cols(j)] = _rot(iq[:, _cols(j)], tab_ref, 3, IDX_ROT // 2).astype(BF16)
    both = proj(SLAB_IW, 2)
    iw_ref[...] = both[:, _cols(0)] * IDX_W_SCALE
    ik_ref[...] = _rot(both[:, _cols(1)], tab_ref, 6, IDX_ROT // 2)[:, :IDX_DIM]

    for j in range(0, d, GATE_COLS):
        ga_ref[:, j:j + GATE_COLS] = jax.nn.sigmoid(
            jnp.dot(hb, wg_ref[:, j:j + GATE_COLS], preferred_element_type=F32))
        gb_ref[:, j:j + GATE_COLS] = jax.nn.sigmoid(
            jnp.dot(hb, wg_ref[:, d + j:d + j + GATE_COLS], preferred_element_type=F32))


def _in_proj(x, tabs, reps, tm, w):
    n, d = x.shape
    r = tabs.shape[1]
    npb = r // tm
    assert r % tm == 0 and n == reps * r
    tok = lambda width: pl.BlockSpec((tm, width), lambda p, b: (b * npb + p, 0))
    full = lambda a: pl.BlockSpec(a.shape, lambda p, b: (0,) * a.ndim)
    outs = [
        ((n, A_KVRANK), F32), ((n, LANES), F32), ((n, B_HEAD_DIM), F32), ((n, B_HEAD_DIM), F32),
        ((n, IDX_DIM), F32), ((n, A_HEADS * LANES), BF16), ((n, B_HEADS * B_HEAD_DIM), BF16),
        ((n, IDX_HEADS * IDX_DIM), BF16), ((n, LANES), F32), ((n, d), F32), ((n, d), F32),
    ]
    return pl.pallas_call(
        _inproj_kernel,
        grid=(npb, reps),
        in_specs=[tok(d), full(w['ln1']), full(w['win']), full(w['wg']), full(w['gq']), full(w['gkv']),
                  full(w['wuq']), full(w['gs']),
                  pl.BlockSpec((tabs.shape[0], tm, LANES), lambda p, b: (0, p, 0))],
        out_specs=[tok(s[1]) for s, _ in outs],
        out_shape=[jax.ShapeDtypeStruct(s, t) for s, t in outs],
        compiler_params=_cparams("arbitrary", "arbitrary"),
        name="in_proj",
    )(x, w['ln1'], w['win'], w['wg'], w['gq'], w['gkv'], w['wuq'], w['gs'], tabs)


def _kvprep_kernel(lat_ref, kpes_ref, wk_ref, wv_ref, gk_ref, ka_ref, va_ref):
    lat = lat_ref[...].astype(BF16)
    kk = jnp.dot(lat, wk_ref[...], preferred_element_type=F32)
    vv = jnp.dot(lat, wv_ref[...], preferred_element_type=F32)
    kpes = kpes_ref[...]
    for hd in range(A_HEADS):
        s = kk[:, _cols(hd)]
        rs = lax.rsqrt(jnp.sum(s * s, axis=-1, keepdims=True) * (1.0 / A_NOPE) + EPS)
        ka_ref[hd] = (s * rs * gk_ref[...] + kpes).astype(BF16)
        va_ref[hd] = vv[:, _cols(hd)].astype(BF16)


def _kv_prep(lat_keys, kpes_keys, w):
    b, s, _ = lat_keys.shape
    ts = _pick_tile(s, KV_PREP_ROWS)
    row = pl.BlockSpec((None, ts, LANES), lambda i, j: (i, j, 0))
    full = lambda a: pl.BlockSpec(a.shape, lambda i, j: (0,) * a.ndim)
    hd = pl.BlockSpec((None, A_HEADS, ts, LANES), lambda i, j: (i, 0, j, 0))
    shp = jax.ShapeDtypeStruct((b, A_HEADS, s, LANES), BF16)
    return pl.pallas_call(
        _kvprep_kernel,
        grid=(b, s // ts),
        in_specs=[row, row, full(w['wk']), full(w['wv']), full(w['gk'])],
        out_specs=[hd, hd],
        out_shape=[shp, shp],
        compiler_params=_cparams("arbitrary", "arbitrary"),
        name="kv_prep",
    )(lat_keys, kpes_keys, w['wk'], w['wv'], w['gk'])


def _mla_kernel(q_ref, k_ref, v_ref, qc_ref, kc_ref, o_ref, bias_s):
    j = pl.program_id(2)

    @pl.when(pl.program_id(1) == 0)
    def _():
        bias_s[j] = jnp.where(kc_ref[...] <= qc_ref[...], 0.0, NEG_BIG)

    for hd in range(MLA_HEADS_PER_STEP):
        s = lax.dot_general(q_ref[:, _cols(hd)], k_ref[hd], NT_DIMS, preferred_element_type=F32)
        s = s + bias_s[j]
        p = jnp.exp2((s - jnp.max(s, axis=-1, keepdims=True)) * (A_SCALE * LOG2_E))
        inv_l = 1.0 / jnp.sum(p, axis=-1, keepdims=True)
        o = jnp.dot(p.astype(BF16), v_ref[hd], preferred_element_type=F32)
        o_ref[:, _cols(hd)] = (o * inv_l).astype(BF16)


def _mla_attn(qa, ka, va, qc, kc, tq, q0, nqg, s):
    n = qa.shape[0]
    b = ka.shape[0]
    nq = n // (b * tq)
    hps = MLA_HEADS_PER_STEP
    qspec = pl.BlockSpec((tq, hps * LANES), lambda i, h, j: (i * nq + q0 + j, h))
    kspec = pl.BlockSpec((None, hps, s, LANES), lambda i, h, j: (i, h, 0, 0))
    return pl.pallas_call(
        _mla_kernel,
        grid=(b, A_HEADS // hps, nqg),
        in_specs=[qspec, kspec, kspec,
                  pl.BlockSpec((tq, 1), lambda i, h, j: (q0 + j, 0)),
                  pl.BlockSpec((1, s), lambda i, h, j: (0, 0))],
        out_specs=pl.BlockSpec((tq, hps * LANES), lambda i, h, j: (i * nqg + j, h)),
        out_shape=jax.ShapeDtypeStruct((b * nqg * tq, A_HEADS * LANES), BF16),
        scratch_shapes=[pltpu.VMEM((nqg, tq, s), F32)],
        compiler_params=_cparams("arbitrary", "arbitrary", "arbitrary"),
        name="mla_attn",
    )(qa, ka, va, qc, kc)


def _dsa_kernel(iq_ref, iw_ref, qb_ref, ik2_ref, bk2_ref, bv2_ref, qc_ref, kc_ref, kidx_ref,
                o_ref, key_s, bias_s, *, k_sel, idx_bits):
    tq = key_s.shape[0]
    adm = kc_ref[...] <= qc_ref[...]
    iw = iw_ref[...]
    score = None
    for j in range(IDX_HEADS // 2):
        iqj = iq_ref[:, _cols(j)]
        for half in range(2):
            hd = 2 * j + half
            rel = jnp.maximum(
                lax.dot_general(iqj, ik2_ref[half], NT_DIMS, preferred_element_type=F32), 0.0)
            term = iw[:, hd:hd + 1] * rel
            score = term if score is None else score + term

    bits = lax.bitcast_convert_type(score, jnp.int32)
    key = jnp.where(bits < 0, bits ^ jnp.int32(0x7FFFFFFF), bits)
    key = jnp.where(key == -1, 0, key)
    key_s[...] = jnp.where(adm, key, INT_MIN)

    kf = float(k_sel)

    def count(mask):
        return jnp.sum(jnp.where(mask, 1.0, 0.0), axis=-1, keepdims=True)

    def value_step(i, cur):
        cand = cur + lax.shift_left(jnp.int32(1), 31 - i)
        return jnp.where(count(key_s[...] >= cand) >= kf, cand, cur)

    tau = lax.fori_loop(0, 32, value_step, jnp.full((tq, 1), INT_MIN, jnp.int32))
    key = key_s[...]
    ge = key >= tau
    c_ge = count(ge)
    bias_s[...] = jnp.where(jnp.logical_and(ge, adm), 0.0, NEG_BIG)
    amb = jnp.logical_and(tau > INT_MIN, c_ge > kf)

    @pl.when(jnp.max(jnp.where(amb, 1.0, 0.0)) > 0.0)
    def _():
        key = key_s[...]
        gt = key > tau
        tie = key == tau
        need = kf - count(gt)
        kidx = kidx_ref[...]

        def index_step(i, cur):
            cand = cur + lax.shift_left(jnp.int32(1), idx_bits - 1 - i)
            below = count(jnp.logical_and(tie, kidx < cand))
            return jnp.where(below < need, cand, cur)

        last = lax.fori_loop(0, idx_bits, index_step, jnp.zeros((tq, 1), jnp.int32))
        sel = jnp.logical_or(gt, jnp.logical_and(tie, kidx <= last))
        bias_s[...] = jnp.where(jnp.logical_and(sel, adm), 0.0, NEG_BIG)

    for j in range(B_HEADS // 2):
        qbj = qb_ref[:, _cols(j)]
        acc = None
        for half in range(2):
            lg = lax.dot_general(qbj, bk2_ref[half], NT_DIMS, preferred_element_type=F32)
            lg = lg + bias_s[...]
            p = jnp.exp(lg - jnp.max(lg, axis=-1, keepdims=True))
            inv_l = 1.0 / jnp.sum(p, axis=-1, keepdims=True)
            o = jnp.dot(p.astype(BF16), bv2_ref[half], preferred_element_type=F32) * inv_l
            acc = o if acc is None else acc + o
        o_ref[:, _cols(j)] = acc.astype(BF16)


def _dsa_attn(iq, iw, qb, ik2, bk2, bv2, qc, kc, kidx, tq, k_sel, q0, nqg, s):
    n = iq.shape[0]
    b, _, s_all, _ = ik2.shape
    nq = n // (b * tq)
    idx_bits = max(1, int(np.ceil(np.log2(s_all + N_META + 1))))
    tok = lambda width: pl.BlockSpec((tq, width), lambda i, j: (i * nq + q0 + j, 0))
    kspec = pl.BlockSpec((None, 2, s, LANES), lambda i, j: (i, 0, 0, 0))
    row = pl.BlockSpec((1, s), lambda i, j: (0, 0))
    return pl.pallas_call(
        functools.partial(_dsa_kernel, k_sel=k_sel, idx_bits=idx_bits),
        grid=(b, nqg),
        in_specs=[tok(iq.shape[1]), tok(LANES), tok(qb.shape[1]), kspec, kspec, kspec,
                  pl.BlockSpec((tq, 1), lambda i, j: (q0 + j, 0)), row, row],
        out_specs=pl.BlockSpec((tq, qb.shape[1]), lambda i, j: (i * nqg + j, 0)),
        out_shape=jax.ShapeDtypeStruct((b * nqg * tq, qb.shape[1]), BF16),
        scratch_shapes=[pltpu.VMEM((tq, s), jnp.int32), pltpu.VMEM((tq, s), F32)],
        compiler_params=_cparams("arbitrary", "arbitrary"),
        name="dsa_attn",
    )(iq, iw, qb, ik2, bk2, bv2, qc, kc, kidx)


def _outproj_kernel(x_ref, oa_ref, ob_ref, ga_ref, gb_ref, woa_ref, wob_ref, wout_ref, ln2_ref,
                    y_ref, t_ref):
    ya = jnp.dot(oa_ref[...], woa_ref[...], preferred_element_type=F32)
    yb = jnp.dot(ob_ref[...], wob_ref[...], preferred_element_type=F32)
    mg = ga_ref[...] * ya + gb_ref[...] * yb
    y = x_ref[...] + jnp.dot(mg.astype(BF16), wout_ref[...], preferred_element_type=F32)
    y_ref[...] = y
    t = y * lax.rsqrt(jnp.mean(y * y, axis=-1, keepdims=True) + EPS) * ln2_ref[...]
    t_ref[...] = t.astype(BF16)


def _out_proj(x, oa, ob, ga, gb, w, tm):
    n, d = x.shape
    tok = lambda a: pl.BlockSpec((tm, a.shape[1]), lambda i: (i, 0))
    full = lambda a: pl.BlockSpec(a.shape, lambda i: (0,) * a.ndim)
    return pl.pallas_call(
        _outproj_kernel,
        grid=(n // tm,),
        in_specs=[tok(x), tok(oa), tok(ob), tok(ga), tok(gb),
                  full(w['woa']), full(w['wob']), full(w['wout']), full(w['ln2'])],
        out_specs=[tok(x), tok(x)],
        out_shape=[jax.ShapeDtypeStruct((n, d), F32), jax.ShapeDtypeStruct((n, d), BF16)],
        compiler_params=_cparams("arbitrary"),
        name="out_proj",
    )(x, oa, ob, ga, gb, w['woa'], w['wob'], w['wout'], w['ln2'])


def _top_rows(s, k, with_rank=False):
    t = s.shape[1]
    row = lax.broadcasted_iota(jnp.int32, (k, t), 0).astype(F32)
    out = jnp.zeros((k, t), F32)
    seen = jnp.zeros((1, t), F32)
    rank = jnp.full(s.shape, float(k), F32)
    for _ in range(k):
        m = jnp.max(s, axis=0, keepdims=True)
        eq = s == m
        if with_rank:
            rank = jnp.where(eq, seen, rank)
        upto = seen + jnp.sum(jnp.where(eq, 1.0, 0.0), axis=0, keepdims=True)
        out = jnp.where(jnp.logical_and(row >= seen, row < upto), m, out)
        seen = upto
        s = jnp.where(eq, -jnp.inf, s)
    return (out, rank) if with_rank else out


def _top_rows_distinct(s, k, with_rank):
    t = s.shape[1]
    row = lax.broadcasted_iota(jnp.int32, (k, t), 0)
    out = jnp.zeros((k, t), F32)
    rank = jnp.full(s.shape, float(k), F32)
    for q in range(k):
        m = jnp.max(s, axis=0, keepdims=True)
        eq = s == m
        if with_rank:
            rank = jnp.where(eq, float(q), rank)
        out = jnp.where(row == q, m, out)
        s = jnp.where(eq, -jnp.inf, s)
    used = jnp.sum(jnp.where(s == -jnp.inf, 1.0, 0.0), axis=0, keepdims=True)
    return out, rank, used


def _top_rows_into(s, k, n_inf, out_s, rank_s=None):
    out, rank, used = _top_rows_distinct(s, k, rank_s is not None)
    out_s[...] = out
    if rank_s is not None:
        rank_s[...] = rank

    @pl.when(jnp.max(used) > float(k + n_inf))
    def _():
        if rank_s is None:
            out_s[...] = _top_rows(s, k)
        else:
            out_s[...], rank_s[...] = _top_rows(s, k, with_rank=True)


def _pair_sum_candidates(a, b):
    k = PEER_TOPK
    assert k == 16 and a.shape[0] == k
    row = lax.broadcasted_iota(jnp.int32, (8, a.shape[1]), 0)
    blocks = [a[0:1] + b]
    n_inf = 0
    for i in range(1, 8):
        blocks.append(jnp.where(row < k // (i + 1), a[i:i + 1] + b[:8], -jnp.inf))
        n_inf += 8 - k // (i + 1)
    blocks.append(a[8:] + b[0:1])
    return jnp.concatenate(blocks, axis=0), n_inf


def _peer_kernel(t_ref, y_ref, wqt_ref, sk1_ref, sk2_ref, u_ref, vt_ref, o_ref,
                 m_s, r_s, a_s, b_s, gw_s, acc_s, qt_s, t1_s, t2_s, tp_s, rk_s):
    c = pl.program_id(1)
    nc = pl.num_programs(1) - 1
    te = u_ref.shape[0]
    half = PEER_DKEY // 2

    @pl.when(c == 0)
    def _route():
        qt = lax.dot_general(wqt_ref[...], t_ref[...], NT_DIMS, preferred_element_type=F32)
        qt_s[...] = qt.astype(BF16)

        @pl.loop(0, PEER_HEADS)
        def _head(hd):
            row0 = pl.multiple_of(hd * PEER_DKEY, PEER_DKEY)
            q1 = qt_s[pl.ds(row0, half), :]
            q2 = qt_s[pl.ds(row0 + half, half), :]
            s1 = jnp.dot(sk1_ref[hd], q1, preferred_element_type=F32)
            s2 = jnp.dot(sk2_ref[hd], q2, preferred_element_type=F32)
            _top_rows_into(s1, PEER_TOPK, 0, t1_s)
            _top_rows_into(s2, PEER_TOPK, 0, t2_s, rk_s)
            top1, top2, r = t1_s[...], t2_s[...], rk_s[...]
            cand, n_inf = _pair_sum_candidates(top1, top2)
            _top_rows_into(cand, PEER_TOPK, n_inf, tp_s)
            top = tp_s[...]
            tau = top[PEER_TOPK - 1:PEER_TOPK]
            z = jnp.sum(jnp.exp(top - top[0:1]), axis=0, keepdims=True)
            m = jnp.zeros(s1.shape, F32)
            for j in range(PEER_TOPK):
                reach = top1 + top2[j:j + 1] >= tau
                c_j = jnp.min(jnp.where(reach, top1, jnp.inf), axis=0, keepdims=True)
                m = m + jnp.where(s1 >= c_j, 1.0, 0.0)
            m_s[hd] = m
            r_s[hd] = r.astype(BF16)
            a_s[hd] = jnp.exp(s1 - top1[0:1])
            b_s[hd] = (jnp.exp(s2 - top2[0:1]) * (1.0 / z)).astype(BF16)
        gw_s[...] = jnp.zeros_like(gw_s)
        acc_s[...] = jnp.zeros_like(acc_s)

    acc_s[...] += jnp.dot(vt_ref[...], gw_s[...], preferred_element_type=F32)
    at = lax.dot_general(u_ref[...], t_ref[...], NT_DIMS, preferred_element_type=F32)
    zero = jnp.zeros((), BF16)
    tm = t_ref.shape[0]
    pk = (N_KEYS // BF16_ROWS, BF16_ROWS, tm)
    lw = min(LANES, tm)
    tok = jnp.zeros((BF16_ROWS, lw), F32)
    for k in range(te // N_KEYS):
        i1 = jnp.minimum(c, nc - 1) * (te // N_KEYS) + k
        w = None
        for hd in range(PEER_HEADS):
            hold = jnp.tile(tok * 0.0, (1, tm // lw))
            m_row = (jnp.broadcast_to(m_s[hd, pl.ds(i1, 1), :], pk[1:]) + hold).astype(BF16)
            a_row = jnp.broadcast_to(a_s[hd, pl.ds(i1, 1), :], pk[1:]).astype(BF16)
            sel = r_s[hd].reshape(pk) < m_row[None]
            term = jnp.where(sel, b_s[hd].reshape(pk) * a_row[None], zero)
            tok = tok + term[0, :, :lw].astype(F32)
            if (k * PEER_HEADS + hd) % PEER_LINK_EVERY == PEER_LINK_EVERY - 1:
                tok = pltpu.roll(tok, 1, 1)
            w = term if w is None else w + term
        g = jax.nn.gelu(at[k * N_KEYS:(k + 1) * N_KEYS])
        gw_s[k * N_KEYS:(k + 1) * N_KEYS, :] = w.reshape(N_KEYS, tm) * g.astype(BF16)

    @pl.when(c == nc)
    def _finish():
        o_ref[...] = y_ref[...] + acc_s[...].T


def _peer(t, y, w, tm, te):
    n, d = y.shape
    nc = w['u'].shape[0] // te
    full = lambda a: pl.BlockSpec(a.shape, lambda i, c: (0,) * a.ndim)
    tok = pl.BlockSpec((tm, d), lambda i, c: (i, 0))
    hk = (PEER_HEADS, N_KEYS, tm)
    return pl.pallas_call(
        _peer_kernel,
        grid=(n // tm, nc + 1),
        in_specs=[tok, tok, full(w['wqt']), full(w['sk1']), full(w['sk2']),
                  pl.BlockSpec((te, d), lambda i, c: (jnp.minimum(c, nc - 1), 0)),
                  pl.BlockSpec((d, te), lambda i, c: (0, jnp.maximum(c - 1, 0)))],
        out_specs=tok,
        out_shape=jax.ShapeDtypeStruct((n, d), F32),
        scratch_shapes=[pltpu.VMEM(hk, F32), pltpu.VMEM(hk, BF16), pltpu.VMEM(hk, F32),
                        pltpu.VMEM(hk, BF16), pltpu.VMEM((te, tm), BF16), pltpu.VMEM((d, tm), F32),
                        pltpu.VMEM((PEER_HEADS * PEER_DKEY, tm), BF16),
                        pltpu.VMEM((PEER_TOPK, tm), F32), pltpu.VMEM((PEER_TOPK, tm), F32),
                        pltpu.VMEM((PEER_TOPK, tm), F32), pltpu.VMEM((N_KEYS, tm), F32)],
        compiler_params=_cparams("arbitrary", "arbitrary"),
        name="peer",
    )(t, y, w['wqt'], w['sk1'], w['sk2'], w['u'], w['vt'])


def _prep_weights(ln1_g, w_in, a_q_norm_g, a_kv_norm_g, a_w_uq, a_w_ukv, a_qk_g, b_qk_g,
                  w_o_a, w_o_b, w_out, ln2_g, peer_wq, peer_subkeys, peer_u, peer_v):
    d = w_in.shape[0]
    offs = np.cumsum(IN_SPLITS).tolist()
    cq, ckv, kpe, bq, bk, bv, iq, iw, ik, gates = jnp.split(w_in, offs, axis=1)
    z = lambda n: jnp.zeros((d, n), w_in.dtype)
    win = jnp.concatenate([
        cq, ckv, z(A_NOPE), kpe, z(LANES - A_NOPE - A_ROPE), bq,
        bk, z(LANES - B_HEAD_DIM), bv, z(LANES - B_HEAD_DIM), iq,
        iw, z(LANES - IDX_HEADS), ik, z(LANES - IDX_DIM)], axis=1).astype(BF16)
    assert win.shape[1] == N_SLAB_SMALL * LANES

    def pad_last(a, n):
        return jnp.pad(a, [(0, 0)] * (a.ndim - 1) + [(0, n - a.shape[-1])])

    wuq = pad_last(a_w_uq.reshape(A_QRANK, A_HEADS, A_NOPE + A_ROPE), LANES)
    wuq = wuq.reshape(A_QRANK, A_HEADS * LANES).astype(BF16)
    ukv = a_w_ukv.reshape(A_KVRANK, A_HEADS, A_NOPE + A_V)
    wk = pad_last(ukv[..., :A_NOPE], LANES).reshape(A_KVRANK, A_HEADS * LANES).astype(BF16)
    wv = pad_last(ukv[..., A_NOPE:], LANES).reshape(A_KVRANK, A_HEADS * LANES).astype(BF16)
    woa = pad_last(w_o_a.reshape(A_HEADS, A_V, d).transpose(0, 2, 1), LANES)
    woa = woa.transpose(0, 2, 1).reshape(A_HEADS * LANES, d).astype(BF16)

    row = lambda v: pad_last(v, LANES)[None, :]
    gs = jnp.concatenate([
        row(a_qk_g[0]),
        row(jnp.concatenate([jnp.zeros((A_NOPE,), F32), a_qk_g[1, A_NOPE:]])),
        row(jnp.concatenate([b_qk_g[0], b_qk_g[0]])),
        row(b_qk_g[1]),
        jnp.zeros((4, LANES), F32)], axis=0)
    return dict(
        ln1=ln1_g[None, :], win=win, wg=gates.astype(BF16), gq=a_q_norm_g[None, :],
        gkv=a_kv_norm_g[None, :], wuq=wuq, gs=gs, wk=wk, wv=wv, gk=row(a_qk_g[1, :A_NOPE]),
        woa=woa, wob=w_o_b.astype(BF16), wout=w_out.astype(BF16), ln2=ln2_g[None, :],
        wqt=peer_wq.T.astype(BF16),
        sk1=peer_subkeys[:, 0].astype(BF16), sk2=peer_subkeys[:, 1].astype(BF16),
        u=peer_u.astype(BF16), vt=peer_v.T.astype(BF16))


def _key_layout(meta, rows):
    pad = jnp.zeros((meta.shape[0], LANES - meta.shape[1], meta.shape[2]), meta.dtype)
    return jnp.concatenate([meta, pad, rows], axis=1)


def _pair_slabs(k):
    kb = k.astype(BF16)
    z = jnp.zeros_like(kb)
    return jnp.stack([jnp.concatenate([kb, z], -1), jnp.concatenate([z, kb], -1)], axis=1)


def _mixers_and_peer(x, rows, keys, qc, kc, kidx, k_sel, w, tq, tm, te, groups):
    lat_k, kpes_k, bk_k, bv_k, ik_k = keys
    (_, _, _, _, _, qa, qb, iq, iw, ga, gb) = rows
    b = lat_k.shape[0]
    ka, va = _kv_prep(lat_k, kpes_k, w)
    ik2, bk2, bv2 = _pair_slabs(ik_k), _pair_slabs(bk_k), _pair_slabs(bv_k)
    oa, ob = [], []
    for q0, nqg, s in groups:
        o = _mla_attn(qa, ka, va, qc, kc, tq, q0, nqg, s)
        oa.append(o.reshape(b, nqg * tq, o.shape[-1]))
        o = _dsa_attn(iq, iw, qb, ik2, bk2, bv2, qc, kc, kidx, tq, k_sel, q0, nqg, s)
        ob.append(o.reshape(b, nqg * tq, o.shape[-1]))
    oa = jnp.concatenate(oa, axis=1).reshape(x.shape[0], -1)
    ob = jnp.concatenate(ob, axis=1).reshape(x.shape[0], -1)
    y1, t2 = _out_proj(x, oa, ob, ga, gb, w, tm)
    return _peer(t2, y1, w, tm, te)


def kernel(x_prompt, x_sample, cache_a_latent, cache_a_kpe, cache_b_k, cache_b_v, cache_b_idx_k,
           meta_tokens, ln1_g, w_in, a_q_norm_g, a_kv_norm_g, a_w_uq, a_w_ukv, a_qk_g, b_qk_g,
           w_o_a, w_o_b, w_out, ln2_g, peer_wq, peer_subkeys, peer_u, peer_v):
    bp, sp, d = x_prompt.shape
    bs, ts, _ = x_sample.shape
    depth, _, past, _ = cache_a_latent.shape
    assert depth == 1 and sp % CHUNK == 0
    w = _prep_weights(ln1_g[0], w_in[0], a_q_norm_g[0], a_kv_norm_g[0], a_w_uq[0], a_w_ukv[0],
                      a_qk_g[0], b_qk_g[0], w_o_a[0], w_o_b[0], w_out[0], ln2_g[0], peer_wq[0],
                      peer_subkeys[0], peer_u[0], peer_v[0])
    tm = 512 if sp % 512 == 0 else 256
    te = 512
    i32 = jnp.int32

    xq = x_prompt.reshape(bp * sp, d)
    rows_p = _in_proj(xq, _all_tables(N_META + jnp.arange(sp, dtype=i32)), bp, tm, w)
    rows_m = _in_proj(meta_tokens.astype(F32), _all_tables(jnp.arange(N_META, dtype=i32)), 1, N_META, w)
    xs = x_sample.reshape(bs * ts, d)
    pos_s = past + jnp.arange(ts, dtype=i32)
    rows_s = _in_proj(xs, _all_tables(jnp.tile(pos_s, bs)), 1, bs * ts, w)

    s_p = LANES + sp
    per_b = lambda a, b, t: a.reshape(b, t, a.shape[-1])
    meta_b = lambda a: jnp.broadcast_to(a[None], (bp,) + a.shape)
    keys_p = tuple(_key_layout(meta_b(m), per_b(r, bp, sp)) for r, m in zip(rows_p[:5], rows_m[:5]))
    chunk_q = jnp.arange(sp, dtype=i32) // CHUNK
    n_pad = LANES - N_META
    kc_p = jnp.concatenate([jnp.full((N_META,), -1, i32), jnp.full((n_pad,), PAD_CHUNK, i32),
                            chunk_q])[None, :]
    kidx_p = jnp.concatenate([jnp.arange(N_META, dtype=i32), s_p + jnp.arange(n_pad, dtype=i32),
                              N_META + jnp.arange(sp, dtype=i32)])[None, :]
    tq = 256
    nq = sp // tq
    tpg = max(1, nq // PROMPT_GROUPS)
    groups_p = [(q0, min(tpg, nq - q0), LANES + min(q0 + tpg, nq) * tq) for q0 in range(0, nq, tpg)]
    y_p = _mixers_and_peer(xq, rows_p, keys_p, chunk_q[:, None], kc_p, kidx_p,
                           min(IDX_TOPK, sp // 4), w, tq, tm, te, groups_p)

    s_s = _round_up(past + ts, LANES)
    kpes_cache = jnp.pad(cache_a_kpe[0], ((0, 0), (0, 0), (A_NOPE, LANES - A_NOPE - A_ROPE)))
    caches = (cache_a_latent[0], kpes_cache, cache_b_k[0], cache_b_v[0], cache_b_idx_k[0])
    keys_s = tuple(jnp.pad(jnp.concatenate([c.astype(F32), per_b(r, bs, ts)], axis=1),
                           ((0, 0), (0, s_s - past - ts), (0, 0)))
                   for c, r in zip(caches, rows_s[:5]))
    kc_s = jnp.concatenate([jnp.zeros((past,), i32), jnp.ones((ts,), i32),
                            jnp.full((s_s - past - ts,), PAD_CHUNK, i32)])[None, :]
    kidx_s = jnp.arange(s_s, dtype=i32)[None, :]
    y_s = _mixers_and_peer(xs, rows_s, keys_s, jnp.ones((ts, 1), i32), kc_s, kidx_s,
                           min(IDX_TOPK, (past + ts) // 4), w, ts, bs * ts, te, [(0, 1, s_s)])

    kpe_of = lambda r: r[:, A_NOPE:A_NOPE + A_ROPE]

    def new_rows_p(r, m):
        return jnp.concatenate([meta_b(m), per_b(r, bp, sp)], axis=1)[None]

    outs_p = [new_rows_p(rows_p[0], rows_m[0]), new_rows_p(kpe_of(rows_p[1]), kpe_of(rows_m[1]))]
    outs_p += [new_rows_p(rows_p[i], rows_m[i]) for i in (2, 3, 4)]
    outs_s = [per_b(rows_s[0], bs, ts)[None], per_b(kpe_of(rows_s[1]), bs, ts)[None]]
    outs_s += [per_b(rows_s[i], bs, ts)[None] for i in (2, 3, 4)]
    return (y_p.reshape(bp, sp, d), y_s.reshape(bs, ts, d), *outs_p, *outs_s)
```

```python
import functools

import jax
import jax.numpy as jnp
import numpy as np
from jax import lax
from jax.experimental import pallas as pl
from jax.experimental.pallas import tpu as pltpu

F32 = jnp.float32
BF16 = jnp.bfloat16

LANES = 128
BF16_ROWS = 16
VMEM_LIMIT = 52 * 1024 * 1024

CHUNK = 64
N_META = 16
ROPE_THETA = 500000.0
EPS = 1e-6

A_HEADS = 8
A_NOPE = 64
A_ROPE = 32
A_V = 64
A_QRANK = 256
A_KVRANK = 128
A_SCALE = (A_NOPE + A_ROPE) ** -0.5
MLA_HEADS_PER_STEP = 4
KV_PREP_ROWS = 1536

B_HEADS = 8
B_HEAD_DIM = 64
B_ROT = B_HEAD_DIM // 4
B_SCALE = B_HEAD_DIM ** -0.5
IDX_HEADS = 8
IDX_DIM = 64
IDX_ROT = IDX_DIM // 4
IDX_TOPK = 256
IDX_W_SCALE = (IDX_HEADS * IDX_DIM) ** -0.5

PEER_HEADS = 8
PEER_DKEY = 128
N_KEYS = 128
PEER_TOPK = 16
PEER_LINK_EVERY = 2
PROMPT_GROUPS = 16

IN_SPLITS = (A_QRANK, A_KVRANK, A_ROPE, B_HEADS * B_HEAD_DIM, B_HEAD_DIM, B_HEAD_DIM,
             IDX_HEADS * IDX_DIM, IDX_HEADS, IDX_DIM)

NEG_BIG = -1e30
LOG2_E = 1.4426950408889634
INT_MIN = -2147483648
PAD_CHUNK = 1 << 30

NT_DIMS = (((1,), (1,)), ((), ()))


def _cparams(*sem):
    return pltpu.CompilerParams(dimension_semantics=sem, vmem_limit_bytes=VMEM_LIMIT)


def _round_up(n, m):
    return -(-n // m) * m


def _pick_tile(n, target):
    best = LANES
    for t in range(LANES, min(n, target) + 1, LANES):
        if n % t == 0:
            best = t
    return best


def _rot_tables(pos, rot_dim, offsets):
    half = rot_dim // 2
    inv = ROPE_THETA ** (-jnp.arange(half, dtype=F32) * (2.0 / rot_dim))
    freq_idx = np.zeros((LANES,), np.int32)
    first = np.zeros((LANES,), bool)
    second = np.zeros((LANES,), bool)
    for o in offsets:
        freq_idx[o:o + rot_dim] = np.arange(rot_dim) % half
        first[o:o + half] = True
        second[o + half:o + rot_dim] = True
    ang = pos.astype(F32)[:, None] * inv[freq_idx][None, :]
    cos, sin = jnp.cos(ang), jnp.sin(ang)
    c = jnp.where((first | second)[None, :], cos, 1.0)
    sm = jnp.where(first[None, :], -sin, 0.0)
    sp = jnp.where(second[None, :], sin, 0.0)
    return jnp.stack([c, sm, sp])


def _all_tables(pos):
    return jnp.concatenate([
        _rot_tables(pos, A_ROPE, (A_NOPE,)),
        _rot_tables(pos, B_ROT, (0, B_HEAD_DIM)),
        _rot_tables(pos, B_ROT, (0,)),
    ])


def _rot(x, tab_ref, base, half):
    c = tab_ref[base]
    sm = tab_ref[base + 1]
    sp = tab_ref[base + 2]
    return x * c + pltpu.roll(x, LANES - half, 1) * sm + pltpu.roll(x, half, 1) * sp


SLAB_CQ, SLAB_CKV, SLAB_KPE, SLAB_BQ, SLAB_BK, SLAB_BV = 0, 2, 3, 4, 8, 9
SLAB_IQ, SLAB_IW, SLAB_IK, SLAB_GA, N_SLAB_SMALL = 10, 14, 15, 16, 16


GATE_COLS = 512


def _cols(s0, n=1):
    return slice(s0 * LANES, (s0 + n) * LANES)


def _inproj_kernel(x_ref, ln1_ref, win_ref, wg_ref, gq_ref, gkv_ref, wuq_ref, gs_ref, tab_ref,
                   lat_ref, kpes_ref, bk_ref, bv_ref, ik_ref, qa_ref, qb_ref, iq_ref, iw_ref,
                   ga_ref, gb_ref):
    d = x_ref.shape[1]
    x = x_ref[...]
    h = x * lax.rsqrt(jnp.mean(x * x, axis=-1, keepdims=True) + EPS) * ln1_ref[...]
    hb = h.astype(BF16)
    lane = lax.broadcasted_iota(jnp.int32, (1, LANES), 1)
    lo = lane < B_HEAD_DIM

    def proj(s0, n=1):
        return jnp.dot(hb, win_ref[:, _cols(s0, n)], preferred_element_type=F32)

    def seg_rs(x2, mask, n):
        return lax.rsqrt(jnp.sum(jnp.where(mask, x2, 0.0), axis=-1, keepdims=True) * (1.0 / n) + EPS)

    cq = proj(SLAB_CQ, 2)
    cq = cq * lax.rsqrt(jnp.mean(cq * cq, axis=-1, keepdims=True) + EPS) * gq_ref[...]
    q = jnp.dot(cq.astype(BF16), wuq_ref[...], preferred_element_type=F32)
    rope_m = jnp.logical_and(lane >= A_NOPE, lane < A_NOPE + A_ROPE)
    for hd in range(A_HEADS):
        s = q[:, _cols(hd)]
        s2 = s * s
        sc = jnp.where(lo, seg_rs(s2, lo, A_NOPE), seg_rs(s2, rope_m, A_ROPE))
        qa_ref[:, _cols(hd)] = _rot(s * sc * gs_ref[0:1, :], tab_ref, 0, A_ROPE // 2).astype(BF16)

    both = proj(SLAB_CKV, 2)
    ckv = both[:, _cols(0)]
    lat_ref[...] = ckv * lax.rsqrt(jnp.mean(ckv * ckv, axis=-1, keepdims=True) + EPS) * gkv_ref[...]
    kp = both[:, _cols(1)]
    kp = kp * seg_rs(kp * kp, rope_m, A_ROPE) * gs_ref[1:2, :]
    kpes_ref[...] = _rot(kp, tab_ref, 0, A_ROPE // 2)

    bq = proj(SLAB_BQ, B_HEADS // 2)
    for j in range(B_HEADS // 2):
        s = bq[:, _cols(j)]
        s2 = s * s
        sc = jnp.where(lo, seg_rs(s2, lo, B_HEAD_DIM), seg_rs(s2, jnp.logical_not(lo), B_HEAD_DIM))
        s = _rot(s * sc * gs_ref[2:3, :], tab_ref, 3, B_ROT // 2) * B_SCALE
        qb_ref[:, _cols(j)] = s.astype(BF16)
    both = proj(SLAB_BK, 2)
    s = both[:, _cols(0)]
    s = s * seg_rs(s * s, lo, B_HEAD_DIM) * gs_ref[3:4, :]
    bk_ref[...] = _rot(s, tab_ref, 6, B_ROT // 2)[:, :B_HEAD_DIM]
    bv_ref[...] = both[:, LANES:LANES + B_HEAD_DIM]

    iq = proj(SLAB_IQ, IDX_HEADS // 2)
    for j in range(IDX_HEADS // 2):
        iq_ref[:, _cols(j)] = _rot(iq[:, _cols(j)], tab_ref, 3, IDX_ROT // 2).astype(BF16)
    both = proj(SLAB_IW, 2)
    iw_ref[...] = both[:, _cols(0)] * IDX_W_SCALE
    ik_ref[...] = _rot(both[:, _cols(1)], tab_ref, 6, IDX_ROT // 2)[:, :IDX_DIM]

    for j in range(0, d, GATE_COLS):
        ga_ref[:, j:j + GATE_COLS] = jax.nn.sigmoid(
            jnp.dot(hb, wg_ref[:, j:j + GATE_COLS], preferred_element_type=F32))
        gb_ref[:, j:j + GATE_COLS] = jax.nn.sigmoid(
            jnp.dot(hb, wg_ref[:, d + j:d + j + GATE_COLS], preferred_element_type=F32))


def _in_proj(x, tabs, reps, tm, w):
    n, d = x.shape
    r = tabs.shape[1]
    npb = r // tm
    assert r % tm == 0 and n == reps * r
    tok = lambda width: pl.BlockSpec((tm, width), lambda p, b: (b * npb + p, 0))
    full = lambda a: pl.BlockSpec(a.shape, lambda p, b: (0,) * a.ndim)
    outs = [
        ((n, A_KVRANK), F32), ((n, LANES), F32), ((n, B_HEAD_DIM), F32), ((n, B_HEAD_DIM), F32),
        ((n, IDX_DIM), F32), ((n, A_HEADS * LANES), BF16), ((n, B_HEADS * B_HEAD_DIM), BF16),
        ((n, IDX_HEADS * IDX_DIM), BF16), ((n, LANES), F32), ((n, d), F32), ((n, d), F32),
    ]
    return pl.pallas_call(
        _inproj_kernel,
        grid=(npb, reps),
        in_specs=[tok(d), full(w['ln1']), full(w['win']), full(w['wg']), full(w['gq']), full(w['gkv']),
                  full(w['wuq']), full(w['gs']),
                  pl.BlockSpec((tabs.shape[0], tm, LANES), lambda p, b: (0, p, 0))],
        out_specs=[tok(s[1]) for s, _ in outs],
        out_shape=[jax.ShapeDtypeStruct(s, t) for s, t in outs],
        compiler_params=_cparams("arbitrary", "arbitrary"),
        name="in_proj",
    )(x, w['ln1'], w['win'], w['wg'], w['gq'], w['gkv'], w['wuq'], w['gs'], tabs)


def _kvprep_kernel(lat_ref, kpes_ref, wk_ref, wv_ref, gk_ref, ka_ref, va_ref):
    lat = lat_ref[...].astype(BF16)
    kk = jnp.dot(lat, wk_ref[...], preferred_element_type=F32)
    vv = jnp.dot(lat, wv_ref[...], preferred_element_type=F32)
    kpes = kpes_ref[...]
    for hd in range(A_HEADS):
        s = kk[:, _cols(hd)]
        rs = lax.rsqrt(jnp.sum(s * s, axis=-1, keepdims=True) * (1.0 / A_NOPE) + EPS)
        ka_ref[hd] = (s * rs * gk_ref[...] + kpes).astype(BF16)
        va_ref[hd] = vv[:, _cols(hd)].astype(BF16)


def _kv_prep(lat_keys, kpes_keys, w):
    b, s, _ = lat_keys.shape
    ts = _pick_tile(s, KV_PREP_ROWS)
    row = pl.BlockSpec((None, ts, LANES), lambda i, j: (i, j, 0))
    full = lambda a: pl.BlockSpec(a.shape, lambda i, j: (0,) * a.ndim)
    hd = pl.BlockSpec((None, A_HEADS, ts, LANES), lambda i, j: (i, 0, j, 0))
    shp = jax.ShapeDtypeStruct((b, A_HEADS, s, LANES), BF16)
    return pl.pallas_call(
        _kvprep_kernel,
        grid=(b, s // ts),
        in_specs=[row, row, full(w['wk']), full(w['wv']), full(w['gk'])],
        out_specs=[hd, hd],
        out_shape=[shp, shp],
        compiler_params=_cparams("arbitrary", "arbitrary"),
        name="kv_prep",
    )(lat_keys, kpes_keys, w['wk'], w['wv'], w['gk'])


def _mla_kernel(q_ref, k_ref, v_ref, qc_ref, kc_ref, o_ref, bias_s):
    j = pl.program_id(2)

    @pl.when(pl.program_id(1) == 0)
    def _():
        bias_s[j] = jnp.where(kc_ref[...] <= qc_ref[...], 0.0, NEG_BIG)

    for hd in range(MLA_HEADS_PER_STEP):
        s = lax.dot_general(q_ref[:, _cols(hd)], k_ref[hd], NT_DIMS, preferred_element_type=F32)
        s = s + bias_s[j]
        p = jnp.exp2((s - jnp.max(s, axis=-1, keepdims=True)) * (A_SCALE * LOG2_E))
        inv_l = 1.0 / jnp.sum(p, axis=-1, keepdims=True)
        o = jnp.dot(p.astype(BF16), v_ref[hd], preferred_element_type=F32)
        o_ref[:, _cols(hd)] = (o * inv_l).astype(BF16)


def _mla_attn(qa, ka, va, qc, kc, tq, q0, nqg, s):
    n = qa.shape[0]
    b = ka.shape[0]
    nq = n // (b * tq)
    hps = MLA_HEADS_PER_STEP
    qspec = pl.BlockSpec((tq, hps * LANES), lambda i, h, j: (i * nq + q0 + j, h))
    kspec = pl.BlockSpec((None, hps, s, LANES), lambda i, h, j: (i, h, 0, 0))
    return pl.pallas_call(
        _mla_kernel,
        grid=(b, A_HEADS // hps, nqg),
        in_specs=[qspec, kspec, kspec,
                  pl.BlockSpec((tq, 1), lambda i, h, j: (q0 + j, 0)),
                  pl.BlockSpec((1, s), lambda i, h, j: (0, 0))],
        out_specs=pl.BlockSpec((tq, hps * LANES), lambda i, h, j: (i * nqg + j, h)),
        out_shape=jax.ShapeDtypeStruct((b * nqg * tq, A_HEADS * LANES), BF16),
        scratch_shapes=[pltpu.VMEM((nqg, tq, s), F32)],
        compiler_params=_cparams("arbitrary", "arbitrary", "arbitrary"),
        name="mla_attn",
    )(qa, ka, va, qc, kc)


def _dsa_kernel(iq_ref, iw_ref, qb_ref, ik2_ref, bk2_ref, bv2_ref, qc_ref, kc_ref, kidx_ref,
                o_ref, key_s, bias_s, *, k_sel, idx_bits):
    tq = key_s.shape[0]
    adm = kc_ref[...] <= qc_ref[...]
    iw = iw_ref[...]
    score = None
    for j in range(IDX_HEADS // 2):
        iqj = iq_ref[:, _cols(j)]
        for half in range(2):
            hd = 2 * j + half
            rel = jnp.maximum(
                lax.dot_general(iqj, ik2_ref[half], NT_DIMS, preferred_element_type=F32), 0.0)
            term = iw[:, hd:hd + 1] * rel
            score = term if score is None else score + term

    bits = lax.bitcast_convert_type(score, jnp.int32)
    key = jnp.where(bits < 0, bits ^ jnp.int32(0x7FFFFFFF), bits)
    key = jnp.where(key == -1, 0, key)
    key_s[...] = jnp.where(adm, key, INT_MIN)

    kf = float(k_sel)

    def count(mask):
        return jnp.sum(jnp.where(mask, 1.0, 0.0), axis=-1, keepdims=True)

    def value_step(i, cur):
        cand = cur + lax.shift_left(jnp.int32(1), 31 - i)
        return jnp.where(count(key_s[...] >= cand) >= kf, cand, cur)

    tau = lax.fori_loop(0, 32, value_step, jnp.full((tq, 1), INT_MIN, jnp.int32))
    key = key_s[...]
    ge = key >= tau
    c_ge = count(ge)
    bias_s[...] = jnp.where(jnp.logical_and(ge, adm), 0.0, NEG_BIG)
    amb = jnp.logical_and(tau > INT_MIN, c_ge > kf)

    @pl.when(jnp.max(jnp.where(amb, 1.0, 0.0)) > 0.0)
    def _():
        key = key_s[...]
        gt = key > tau
        tie = key == tau
        need = kf - count(gt)
        kidx = kidx_ref[...]

        def index_step(i, cur):
            cand = cur + lax.shift_left(jnp.int32(1), idx_bits - 1 - i)
            below = count(jnp.logical_and(tie, kidx < cand))
            return jnp.where(below < need, cand, cur)

        last = lax.fori_loop(0, idx_bits, index_step, jnp.zeros((tq, 1), jnp.int32))
        sel = jnp.logical_or(gt, jnp.logical_and(tie, kidx <= last))
        bias_s[...] = jnp.where(jnp.logical_and(sel, adm), 0.0, NEG_BIG)

    for j in range(B_HEADS // 2):
        qbj = qb_ref[:, _cols(j)]
        acc = None
        for half in range(2):
            lg = lax.dot_general(qbj, bk2_ref[half], NT_DIMS, preferred_element_type=F32)
            lg = lg + bias_s[...]
            p = jnp.exp(lg - jnp.max(lg, axis=-1, keepdims=True))
            inv_l = 1.0 / jnp.sum(p, axis=-1, keepdims=True)
            o = jnp.dot(p.astype(BF16), bv2_ref[half], preferred_element_type=F32) * inv_l
            acc = o if acc is None else acc + o
        o_ref[:, _cols(j)] = acc.astype(BF16)


def _dsa_attn(iq, iw, qb, ik2, bk2, bv2, qc, kc, kidx, tq, k_sel, q0, nqg, s):
    n = iq.shape[0]
    b, _, s_all, _ = ik2.shape
    nq = n // (b * tq)
    idx_bits = max(1, int(np.ceil(np.log2(s_all + N_META + 1))))
    tok = lambda width: pl.BlockSpec((tq, width), lambda i, j: (i * nq + q0 + j, 0))
    kspec = pl.BlockSpec((None, 2, s, LANES), lambda i, j: (i, 0, 0, 0))
    row = pl.BlockSpec((1, s), lambda i, j: (0, 0))
    return pl.pallas_call(
        functools.partial(_dsa_kernel, k_sel=k_sel, idx_bits=idx_bits),
        grid=(b, nqg),
        in_specs=[tok(iq.shape[1]), tok(LANES), tok(qb.shape[1]), kspec, kspec, kspec,
                  pl.BlockSpec((tq, 1), lambda i, j: (q0 + j, 0)), row, row],
        out_specs=pl.BlockSpec((tq, qb.shape[1]), lambda i, j: (i * nqg + j, 0)),
        out_shape=jax.ShapeDtypeStruct((b * nqg * tq, qb.shape[1]), BF16),
        scratch_shapes=[pltpu.VMEM((tq, s), jnp.int32), pltpu.VMEM((tq, s), F32)],
        compiler_params=_cparams("arbitrary", "arbitrary"),
        name="dsa_attn",
    )(iq, iw, qb, ik2, bk2, bv2, qc, kc, kidx)


def _outproj_kernel(x_ref, oa_ref, ob_ref, ga_ref, gb_ref, woa_ref, wob_ref, wout_ref, ln2_ref,
                    y_ref, t_ref):
    ya = jnp.dot(oa_ref[...], woa_ref[...], preferred_element_type=F32)
    yb = jnp.dot(ob_ref[...], wob_ref[...], preferred_element_type=F32)
    mg = ga_ref[...] * ya + gb_ref[...] * yb
    y = x_ref[...] + jnp.dot(mg.astype(BF16), wout_ref[...], preferred_element_type=F32)
    y_ref[...] = y
    t = y * lax.rsqrt(jnp.mean(y * y, axis=-1, keepdims=True) + EPS) * ln2_ref[...]
    t_ref[...] = t.astype(BF16)


def _out_proj(x, oa, ob, ga, gb, w, tm):
    n, d = x.shape
    tok = lambda a: pl.BlockSpec((tm, a.shape[1]), lambda i: (i, 0))
    full = lambda a: pl.BlockSpec(a.shape, lambda i: (0,) * a.ndim)
    return pl.pallas_call(
        _outproj_kernel,
        grid=(n // tm,),
        in_specs=[tok(x), tok(oa), tok(ob), tok(ga), tok(gb),
                  full(w['woa']), full(w['wob']), full(w['wout']), full(w['ln2'])],
        out_specs=[tok(x), tok(x)],
        out_shape=[jax.ShapeDtypeStruct((n, d), F32), jax.ShapeDtypeStruct((n, d), BF16)],
        compiler_params=_cparams("arbitrary"),
        name="out_proj",
    )(x, oa, ob, ga, gb, w['woa'], w['wob'], w['wout'], w['ln2'])


def _gelu_tanh(x):
    k0 = -2.0 * np.sqrt(2.0 / np.pi) * LOG2_E
    return x / (1.0 + jnp.exp2(x * (k0 + (k0 * 0.044715) * (x * x))))


def _top_rows(s, k, with_rank=False):
    t = s.shape[1]
    row = lax.broadcasted_iota(jnp.int32, (k, t), 0).astype(F32)
    out = jnp.zeros((k, t), F32)
    seen = jnp.zeros((1, t), F32)
    rank = jnp.full(s.shape, float(k), F32)
    for _ in range(k):
        m = jnp.max(s, axis=0, keepdims=True)
        eq = s == m
        if with_rank:
            rank = jnp.where(eq, seen, rank)
        upto = seen + jnp.sum(jnp.where(eq, 1.0, 0.0), axis=0, keepdims=True)
        out = jnp.where(jnp.logical_and(row >= seen, row < upto), m, out)
        seen = upto
        s = jnp.where(eq, -jnp.inf, s)
    return (out, rank) if with_rank else out


def _top_rows_distinct(s, k, with_rank):
    t = s.shape[1]
    row = lax.broadcasted_iota(jnp.int32, (k, t), 0)
    out = jnp.zeros((k, t), F32)
    rank = jnp.full(s.shape, float(k), F32)
    for q in range(k):
        m = jnp.max(s, axis=0, keepdims=True)
        eq = s == m
        if with_rank:
            rank = jnp.where(eq, float(q), rank)
        out = jnp.where(row == q, m, out)
        s = jnp.where(eq, -jnp.inf, s)
    used = jnp.sum(jnp.where(s == -jnp.inf, 1.0, 0.0), axis=0, keepdims=True)
    return out, rank, used


def _top_rows_into(s, k, n_inf, out_s, rank_s=None):
    out, rank, used = _top_rows_distinct(s, k, rank_s is not None)
    out_s[...] = out
    if rank_s is not None:
        rank_s[...] = rank

    @pl.when(jnp.max(used) > float(k + n_inf))
    def _():
        if rank_s is None:
            out_s[...] = _top_rows(s, k)
        else:
            out_s[...], rank_s[...] = _top_rows(s, k, with_rank=True)


def _pair_sum_candidates(a, b):
    k = PEER_TOPK
    assert k == 16 and a.shape[0] == k
    row = lax.broadcasted_iota(jnp.int32, (8, a.shape[1]), 0)
    blocks = [a[0:1] + b]
    n_inf = 0
    for i in range(1, 8):
        blocks.append(jnp.where(row < k // (i + 1), a[i:i + 1] + b[:8], -jnp.inf))
        n_inf += 8 - k // (i + 1)
    blocks.append(a[8:] + b[0:1])
    return jnp.concatenate(blocks, axis=0), n_inf


def _peer_kernel(t_ref, y_ref, wqt_ref, sk1_ref, sk2_ref, u_ref, vt_ref, o_ref,
                 m_s, r_s, a_s, b_s, gw_s, acc_s, qt_s, t1_s, t2_s, tp_s, rk_s):
    c = pl.program_id(1)
    nc = pl.num_programs(1) - 1
    te = u_ref.shape[0]
    half = PEER_DKEY // 2

    @pl.when(c == 0)
    def _route():
        qt = lax.dot_general(wqt_ref[...], t_ref[...], NT_DIMS, preferred_element_type=F32)
        qt_s[...] = qt.astype(BF16)

        @pl.loop(0, PEER_HEADS)
        def _head(hd):
            row0 = pl.multiple_of(hd * PEER_DKEY, PEER_DKEY)
            q1 = qt_s[pl.ds(row0, half), :]
            q2 = qt_s[pl.ds(row0 + half, half), :]
            s1 = jnp.dot(sk1_ref[hd], q1, preferred_element_type=F32)
            s2 = jnp.dot(sk2_ref[hd], q2, preferred_element_type=F32)
            _top_rows_into(s1, PEER_TOPK, 0, t1_s)
            _top_rows_into(s2, PEER_TOPK, 0, t2_s, rk_s)
            top1, top2, r = t1_s[...], t2_s[...], rk_s[...]
            cand, n_inf = _pair_sum_candidates(top1, top2)
            _top_rows_into(cand, PEER_TOPK, n_inf, tp_s)
            top = tp_s[...]
            tau = top[PEER_TOPK - 1:PEER_TOPK]
            z = jnp.sum(jnp.exp(top - top[0:1]), axis=0, keepdims=True)
            m = jnp.zeros(s1.shape, F32)
            for j in range(PEER_TOPK):
                reach = top1 + top2[j:j + 1] >= tau
                c_j = jnp.min(jnp.where(reach, top1, jnp.inf), axis=0, keepdims=True)
                m = m + jnp.where(s1 >= c_j, 1.0, 0.0)
            m_s[hd] = m
            r_s[hd] = r.astype(BF16)
            a_s[hd] = jnp.exp(s1 - top1[0:1])
            b_s[hd] = (jnp.exp(s2 - top2[0:1]) * (1.0 / z)).astype(BF16)
        gw_s[...] = jnp.zeros_like(gw_s)
        acc_s[...] = jnp.zeros_like(acc_s)

    acc_s[...] += jnp.dot(vt_ref[...], gw_s[...], preferred_element_type=F32)
    at = lax.dot_general(u_ref[...], t_ref[...], NT_DIMS, preferred_element_type=F32)
    zero = jnp.zeros((), BF16)
    tm = t_ref.shape[0]
    pk = (N_KEYS // BF16_ROWS, BF16_ROWS, tm)
    lw = min(LANES, tm)
    tok = jnp.zeros((BF16_ROWS, lw), F32)
    for k in range(te // N_KEYS):
        i1 = jnp.minimum(c, nc - 1) * (te // N_KEYS) + k
        w = None
        for hd in range(PEER_HEADS):
            hold = jnp.tile(tok * 0.0, (1, tm // lw))
            m_row = (jnp.broadcast_to(m_s[hd, pl.ds(i1, 1), :], pk[1:]) + hold).astype(BF16)
            a_row = jnp.broadcast_to(a_s[hd, pl.ds(i1, 1), :], pk[1:]).astype(BF16)
            sel = r_s[hd].reshape(pk) < m_row[None]
            term = jnp.where(sel, b_s[hd].reshape(pk) * a_row[None], zero)
            tok = tok + term[0, :, :lw].astype(F32)
            if (k * PEER_HEADS + hd) % PEER_LINK_EVERY == PEER_LINK_EVERY - 1:
                tok = pltpu.roll(tok, 1, 1)
            w = term if w is None else w + term
        g = _gelu_tanh(at[k * N_KEYS:(k + 1) * N_KEYS])
        gw_s[k * N_KEYS:(k + 1) * N_KEYS, :] = w.reshape(N_KEYS, tm) * g.astype(BF16)

    @pl.when(c == nc)
    def _finish():
        o_ref[...] = y_ref[...] + acc_s[...].T


def _peer(t, y, w, tm, te):
    n, d = y.shape
    nc = w['u'].shape[0] // te
    full = lambda a: pl.BlockSpec(a.shape, lambda i, c: (0,) * a.ndim)
    tok = pl.BlockSpec((tm, d), lambda i, c: (i, 0))
    hk = (PEER_HEADS, N_KEYS, tm)
    return pl.pallas_call(
        _peer_kernel,
        grid=(n // tm, nc + 1),
        in_specs=[tok, tok, full(w['wqt']), full(w['sk1']), full(w['sk2']),
                  pl.BlockSpec((te, d), lambda i, c: (jnp.minimum(c, nc - 1), 0)),
                  pl.BlockSpec((d, te), lambda i, c: (0, jnp.maximum(c - 1, 0)))],
        out_specs=tok,
        out_shape=jax.ShapeDtypeStruct((n, d), F32),
        scratch_shapes=[pltpu.VMEM(hk, F32), pltpu.VMEM(hk, BF16), pltpu.VMEM(hk, F32),
                        pltpu.VMEM(hk, BF16), pltpu.VMEM((te, tm), BF16), pltpu.VMEM((d, tm), F32),
                        pltpu.VMEM((PEER_HEADS * PEER_DKEY, tm), BF16),
                        pltpu.VMEM((PEER_TOPK, tm), F32), pltpu.VMEM((PEER_TOPK, tm), F32),
                        pltpu.VMEM((PEER_TOPK, tm), F32), pltpu.VMEM((N_KEYS, tm), F32)],
        compiler_params=_cparams("arbitrary", "arbitrary"),
        name="peer",
    )(t, y, w['wqt'], w['sk1'], w['sk2'], w['u'], w['vt'])


def _prep_weights(ln1_g, w_in, a_q_norm_g, a_kv_norm_g, a_w_uq, a_w_ukv, a_qk_g, b_qk_g,
                  w_o_a, w_o_b, w_out, ln2_g, peer_wq, peer_subkeys, peer_u, peer_v):
    d = w_in.shape[0]
    offs = np.cumsum(IN_SPLITS).tolist()
    cq, ckv, kpe, bq, bk, bv, iq, iw, ik, gates = jnp.split(w_in, offs, axis=1)
    z = lambda n: jnp.zeros((d, n), w_in.dtype)
    win = jnp.concatenate([
        cq, ckv, z(A_NOPE), kpe, z(LANES - A_NOPE - A_ROPE), bq,
        bk, z(LANES - B_HEAD_DIM), bv, z(LANES - B_HEAD_DIM), iq,
        iw, z(LANES - IDX_HEADS), ik, z(LANES - IDX_DIM)], axis=1).astype(BF16)
    assert win.shape[1] == N_SLAB_SMALL * LANES

    def pad_last(a, n):
        return jnp.pad(a, [(0, 0)] * (a.ndim - 1) + [(0, n - a.shape[-1])])

    wuq = pad_last(a_w_uq.reshape(A_QRANK, A_HEADS, A_NOPE + A_ROPE), LANES)
    wuq = wuq.reshape(A_QRANK, A_HEADS * LANES).astype(BF16)
    ukv = a_w_ukv.reshape(A_KVRANK, A_HEADS, A_NOPE + A_V)
    wk = pad_last(ukv[..., :A_NOPE], LANES).reshape(A_KVRANK, A_HEADS * LANES).astype(BF16)
    wv = pad_last(ukv[..., A_NOPE:], LANES).reshape(A_KVRANK, A_HEADS * LANES).astype(BF16)
    woa = pad_last(w_o_a.reshape(A_HEADS, A_V, d).transpose(0, 2, 1), LANES)
    woa = woa.transpose(0, 2, 1).reshape(A_HEADS * LANES, d).astype(BF16)

    row = lambda v: pad_last(v, LANES)[None, :]
    gs = jnp.concatenate([
        row(a_qk_g[0]),
        row(jnp.concatenate([jnp.zeros((A_NOPE,), F32), a_qk_g[1, A_NOPE:]])),
        row(jnp.concatenate([b_qk_g[0], b_qk_g[0]])),
        row(b_qk_g[1]),
        jnp.zeros((4, LANES), F32)], axis=0)
    return dict(
        ln1=ln1_g[None, :], win=win, wg=gates.astype(BF16), gq=a_q_norm_g[None, :],
        gkv=a_kv_norm_g[None, :], wuq=wuq, gs=gs, wk=wk, wv=wv, gk=row(a_qk_g[1, :A_NOPE]),
        woa=woa, wob=w_o_b.astype(BF16), wout=w_out.astype(BF16), ln2=ln2_g[None, :],
        wqt=peer_wq.T.astype(BF16),
        sk1=peer_subkeys[:, 0].astype(BF16), sk2=peer_subkeys[:, 1].astype(BF16),
        u=peer_u.astype(BF16), vt=peer_v.T.astype(BF16))


def _key_layout(meta, rows):
    pad = jnp.zeros((meta.shape[0], LANES - meta.shape[1], meta.shape[2]), meta.dtype)
    return jnp.concatenate([meta, pad, rows], axis=1)


def _pair_slabs(k):
    kb = k.astype(BF16)
    z = jnp.zeros_like(kb)
    return jnp.stack([jnp.concatenate([kb, z], -1), jnp.concatenate([z, kb], -1)], axis=1)


def _mixers_and_peer(x, rows, keys, qc, kc, kidx, k_sel, w, tq, tm, te, groups):
    lat_k, kpes_k, bk_k, bv_k, ik_k = keys
    (_, _, _, _, _, qa, qb, iq, iw, ga, gb) = rows
    b = lat_k.shape[0]
    ka, va = _kv_prep(lat_k, kpes_k, w)
    ik2, bk2, bv2 = _pair_slabs(ik_k), _pair_slabs(bk_k), _pair_slabs(bv_k)
    oa, ob = [], []
    for q0, nqg, s in groups:
        o = _mla_attn(qa, ka, va, qc, kc, tq, q0, nqg, s)
        oa.append(o.reshape(b, nqg * tq, o.shape[-1]))
        o = _dsa_attn(iq, iw, qb, ik2, bk2, bv2, qc, kc, kidx, tq, k_sel, q0, nqg, s)
        ob.append(o.reshape(b, nqg * tq, o.shape[-1]))
    oa = jnp.concatenate(oa, axis=1).reshape(x.shape[0], -1)
    ob = jnp.concatenate(ob, axis=1).reshape(x.shape[0], -1)
    y1, t2 = _out_proj(x, oa, ob, ga, gb, w, tm)
    return _peer(t2, y1, w, tm, te)


def kernel(x_prompt, x_sample, cache_a_latent, cache_a_kpe, cache_b_k, cache_b_v, cache_b_idx_k,
           meta_tokens, ln1_g, w_in, a_q_norm_g, a_kv_norm_g, a_w_uq, a_w_ukv, a_qk_g, b_qk_g,
           w_o_a, w_o_b, w_out, ln2_g, peer_wq, peer_subkeys, peer_u, peer_v):
    bp, sp, d = x_prompt.shape
    bs, ts, _ = x_sample.shape
    depth, _, past, _ = cache_a_latent.shape
    assert depth == 1 and sp % CHUNK == 0
    w = _prep_weights(ln1_g[0], w_in[0], a_q_norm_g[0], a_kv_norm_g[0], a_w_uq[0], a_w_ukv[0],
                      a_qk_g[0], b_qk_g[0], w_o_a[0], w_o_b[0], w_out[0], ln2_g[0], peer_wq[0],
                      peer_subkeys[0], peer_u[0], peer_v[0])
    tm = 512 if sp % 512 == 0 else 256
    te = 512
    i32 = jnp.int32

    xq = x_prompt.reshape(bp * sp, d)
    rows_p = _in_proj(xq, _all_tables(N_META + jnp.arange(sp, dtype=i32)), bp, tm, w)
    rows_m = _in_proj(meta_tokens.astype(F32), _all_tables(jnp.arange(N_META, dtype=i32)), 1, N_META, w)
    xs = x_sample.reshape(bs * ts, d)
    pos_s = past + jnp.arange(ts, dtype=i32)
    rows_s = _in_proj(xs, _all_tables(jnp.tile(pos_s, bs)), 1, bs * ts, w)

    s_p = LANES + sp
    per_b = lambda a, b, t: a.reshape(b, t, a.shape[-1])
    meta_b = lambda a: jnp.broadcast_to(a[None], (bp,) + a.shape)
    keys_p = tuple(_key_layout(meta_b(m), per_b(r, bp, sp)) for r, m in zip(rows_p[:5], rows_m[:5]))
    chunk_q = jnp.arange(sp, dtype=i32) // CHUNK
    n_pad = LANES - N_META
    kc_p = jnp.concatenate([jnp.full((N_META,), -1, i32), jnp.full((n_pad,), PAD_CHUNK, i32),
                            chunk_q])[None, :]
    kidx_p = jnp.concatenate([jnp.arange(N_META, dtype=i32), s_p + jnp.arange(n_pad, dtype=i32),
                              N_META + jnp.arange(sp, dtype=i32)])[None, :]
    tq = 256
    nq = sp // tq
    tpg = max(1, nq // PROMPT_GROUPS)
    groups_p = [(q0, min(tpg, nq - q0), LANES + min(q0 + tpg, nq) * tq) for q0 in range(0, nq, tpg)]
    y_p = _mixers_and_peer(xq, rows_p, keys_p, chunk_q[:, None], kc_p, kidx_p,
                           min(IDX_TOPK, sp // 4), w, tq, tm, te, groups_p)

    s_s = _round_up(past + ts, LANES)
    kpes_cache = jnp.pad(cache_a_kpe[0], ((0, 0), (0, 0), (A_NOPE, LANES - A_NOPE - A_ROPE)))
    caches = (cache_a_latent[0], kpes_cache, cache_b_k[0], cache_b_v[0], cache_b_idx_k[0])
    keys_s = tuple(jnp.pad(jnp.concatenate([c.astype(F32), per_b(r, bs, ts)], axis=1),
                           ((0, 0), (0, s_s - past - ts), (0, 0)))
                   for c, r in zip(caches, rows_s[:5]))
    kc_s = jnp.concatenate([jnp.zeros((past,), i32), jnp.ones((ts,), i32),
                            jnp.full((s_s - past - ts,), PAD_CHUNK, i32)])[None, :]
    kidx_s = jnp.arange(s_s, dtype=i32)[None, :]
    y_s = _mixers_and_peer(xs, rows_s, keys_s, jnp.ones((ts, 1), i32), kc_s, kidx_s,
                           min(IDX_TOPK, (past + ts) // 4), w, ts, bs * ts, te, [(0, 1, s_s)])

    kpe_of = lambda r: r[:, A_NOPE:A_NOPE + A_ROPE]

    def new_rows_p(r, m):
        return jnp.concatenate([meta_b(m), per_b(r, bp, sp)], axis=1)[None]

    outs_p = [new_rows_p(rows_p[0], rows_m[0]), new_rows_p(kpe_of(rows_p[1]), kpe_of(rows_m[1]))]
    outs_p += [new_rows_p(rows_p[i], rows_m[i]) for i in (2, 3, 4)]
    outs_s = [per_b(rows_s[0], bs, ts)[None], per_b(kpe_of(rows_s[1]), bs, ts)[None]]
    outs_s += [per_b(rows_s[i], bs, ts)[None] for i in (2, 3, 4)]
    return (y_p.reshape(bp, sp, d), y_s.reshape(bs, ts, d), *outs_p, *outs_s)
```
